```python
import jax, jax.numpy as jnp
from jax import lax
import numpy as np

D_MODEL = 1024
BATCH = 32
SEQ = 2048
DEPTH = 4

CHUNK = 64
RMS_EPS = 1e-6
M_HEADS = 4
M_HEAD_DIM = D_MODEL // 4
M_WIDTH = M_HEADS * M_HEAD_DIM
CONV_K = 4
P_GROUPS = 4
P_WINDOWS = (2, 4, 8, 16)
P_GROUP_DIM = D_MODEL // 4
P_WIDTH = P_GROUPS * P_GROUP_DIM
A_IN = 3 * M_WIDTH + 2 * M_HEADS + 2 * P_WIDTH
A_OUT_IN = M_WIDTH + P_WIDTH
C_HEADS = 8
QK_NOPE = 128
QK_ROPE = 64
QK_HEAD = QK_NOPE + QK_ROPE
V_HEAD = 128
Q_LORA = 384
KV_LORA = 256
C_WIDTH = C_HEADS * V_HEAD
C_IN = Q_LORA + KV_LORA + QK_ROPE + C_WIDTH
ROPE_THETA = 10000.0
Q_BLOCK = 128
N_A = (DEPTH + 1) // 2
N_C = DEPTH // 2

kernel_name = 'hybrid_mlstm_pool_mla_stream'


def rms_norm(x, g):
    xf = x.astype(jnp.float32)
    y = xf * lax.rsqrt(jnp.mean(xf * xf, axis=-1, keepdims=True) + RMS_EPS)
    return (y * g.astype(jnp.float32)).astype(x.dtype)


def causal_conv(x, w, b):
    y = lax.conv_general_dilated(x, w[:, None, :].astype(x.dtype), window_strides=(1,),
                                 padding=[(CONV_K - 1, 0)],
                                 dimension_numbers=('NWC', 'WIO', 'NWC'),
                                 feature_group_count=x.shape[-1])
    return y + b


def mlstm_chunkwise(q, k, v, i_pre, log_f):
    B, H, S, Dk = q.shape
    Dv = v.shape[-1]
    nc = S // CHUNK

    def to_chunks(a):
        return jnp.moveaxis(a.reshape(B, H, nc, CHUNK, *a.shape[3:]), 2, 0)

    xs = (to_chunks(q), to_chunks(k), to_chunks(v), to_chunks(i_pre), to_chunks(log_f))
    tri = jnp.tril(jnp.ones((CHUNK, CHUNK), dtype=bool))

    def step(carry, inp):
        C, n, m = carry
        qb, kb, vb, ib, fb = inp
        bcum = jnp.cumsum(fb, axis=-1)
        dmat = jnp.where(tri, bcum[..., :, None] - bcum[..., None, :] + ib[..., None, :], -jnp.inf)
        inter = bcum + m[..., None]
        m_t = jnp.maximum(inter, jnp.max(dmat, axis=-1))
        w_inter = jnp.exp(inter - m_t)
        s = jnp.einsum('bhtd,bhsd->bhts', qb, kb) * jnp.exp(dmat - m_t[..., None])
        num = jnp.einsum('bhts,bhsv->bhtv', s, vb) + w_inter[..., None] * jnp.einsum('bhtd,bhdv->bhtv', qb, C)
        den = jnp.sum(s, axis=-1) + w_inter * jnp.einsum('bhtd,bhd->bht', qb, n)
        h = num / jnp.maximum(jnp.abs(den), jnp.exp(-m_t))[..., None]
        b_last = bcum[..., -1]
        g = b_last[..., None] - bcum + ib
        m_new = jnp.maximum(b_last + m, jnp.max(g, axis=-1))
        ws = jnp.exp(g - m_new[..., None])
        wc = jnp.exp(b_last + m - m_new)
        kw = kb * ws[..., None]
        C_new = wc[..., None, None] * C + jnp.einsum('bhsd,bhsv->bhdv', kw, vb)
        n_new = wc[..., None] * n + jnp.sum(kw, axis=2)
        return (C_new, n_new, m_new), h

    init = (jnp.zeros((B, H, Dk, Dv), jnp.float32), jnp.zeros((B, H, Dk), jnp.float32),
            jnp.zeros((B, H), jnp.float32))
    _, hs = lax.scan(step, init, xs)
    return jnp.moveaxis(hs, 0, 2).reshape(B, H, S, Dv)


def multiscale_pool(xp):
    B, S, _ = xp.shape
    xf = xp.astype(jnp.float32)
    cs = jnp.pad(jnp.cumsum(xf, axis=1), ((0, 0), (1, 0), (0, 0)))
    t = jnp.arange(S)
    means = []
    for g, w in enumerate(P_WINDOWS):
        csg = cs[..., g * P_GROUP_DIM:(g + 1) * P_GROUP_DIM]
        lo = jnp.maximum(t + 1 - w, 0)
        cnt = jnp.minimum(t + 1, w).astype(jnp.float32)
        means.append((csg[:, 1:] - jnp.take(csg, lo, axis=1)) / cnt[None, :, None])
    return (jnp.concatenate(means, axis=-1) - xf).astype(xp.dtype)


def mlstm_pool_layer(x, norm_g, w_in, b_if, conv_w, conv_b, w_q, w_k, w_v, head_norm_g, skip,
                     pool_w, pool_scale, w_out):
    B, S, _ = x.shape
    u = rms_norm(x, norm_g) @ w_in
    s1 = M_WIDTH
    s2 = 2 * M_WIDTH
    s3 = s2 + 2 * M_HEADS
    s4 = s3 + M_WIDTH
    s5 = s4 + P_WIDTH
    xm, om, gates, zm, xp, zp = jnp.split(u, [s1, s2, s3, s4, s5], axis=-1)
    xc = jax.nn.silu(causal_conv(xm, conv_w, conv_b))
    xc_h = xc.reshape(B, S, M_HEADS, M_HEAD_DIM)
    xm_h = xm.reshape(B, S, M_HEADS, M_HEAD_DIM)
    q = jnp.einsum('bshc,hcd->bhsd', xc_h, w_q).astype(jnp.float32)
    k = jnp.einsum('bshc,hcd->bhsd', xc_h, w_k).astype(jnp.float32) * (M_HEAD_DIM ** -0.5)
    v = jnp.einsum('bshc,hcd->bhsd', xm_h, w_v).astype(jnp.float32)
    gates = (gates + b_if).astype(jnp.float32)
    i_pre = jnp.moveaxis(gates[..., :M_HEADS], -1, 1)
    log_f = jnp.moveaxis(jax.nn.log_sigmoid(gates[..., M_HEADS:]), -1, 1)
    hm = mlstm_chunkwise(q, k, v, i_pre, log_f)
    hm = jnp.moveaxis(hm, 1, 2).reshape(B, S, M_WIDTH).astype(x.dtype) * jax.nn.sigmoid(om)
    hm = rms_norm(hm.reshape(B, S, M_HEADS, M_HEAD_DIM),
                  head_norm_g.reshape(M_HEADS, M_HEAD_DIM)).reshape(B, S, M_WIDTH)
    ya = (hm + skip * xc) * jax.nn.silu(zm)
    mix = multiscale_pool(xp).reshape(B, S, P_GROUPS, P_GROUP_DIM)
    yb = jnp.einsum('bsgc,gcd->bsgd', mix, pool_w).reshape(B, S, P_WIDTH) * pool_scale * jax.nn.silu(zp)
    return x + jnp.concatenate([ya, yb], axis=-1) @ w_out


def rope(x, cos, sin):
    x1, x2 = jnp.split(x.astype(jnp.float32), 2, axis=-1)
    return jnp.concatenate([x1 * cos - x2 * sin, x2 * cos + x1 * sin], axis=-1).astype(x.dtype)


def block_causal_attention(q, k, v):
    B, S, H, _ = q.shape
    scale = QK_HEAD ** -0.5
    chunk_id = jnp.arange(S) // CHUNK
    outs = []
    for qb in range(S // Q_BLOCK):
        q0, q1 = qb * Q_BLOCK, (qb + 1) * Q_BLOCK
        s = jnp.einsum('bqhd,bkhd->bhqk', q[:, q0:q1], k[:, :q1],
                       preferred_element_type=jnp.float32) * scale
        mask = chunk_id[q0:q1, None] >= chunk_id[None, :q1]
        p = jax.nn.softmax(jnp.where(mask, s, -jnp.inf), axis=-1).astype(v.dtype)
        outs.append(jnp.einsum('bhqk,bkhd->bqhd', p, v[:, :q1]))
    return jnp.concatenate(outs, axis=1)


def mla_layer(x, positions, norm_g, w_in, q_norm_g, kv_norm_g, w_uq, w_ukv, qn_g, kn_g, w_out):
    B, S, _ = x.shape
    u = rms_norm(x, norm_g) @ w_in
    cq, ckv, kpe, z = jnp.split(u, [Q_LORA, Q_LORA + KV_LORA, Q_LORA + KV_LORA + QK_ROPE], axis=-1)
    q = (rms_norm(cq, q_norm_g) @ w_uq).reshape(B, S, C_HEADS, QK_HEAD)
    kv = (rms_norm(ckv, kv_norm_g) @ w_ukv).reshape(B, S, C_HEADS, QK_NOPE + V_HEAD)
    k_nope, v = jnp.split(kv, [QK_NOPE], axis=-1)
    k = jnp.concatenate([k_nope, jnp.broadcast_to(kpe[:, :, None, :], (B, S, C_HEADS, QK_ROPE))], axis=-1)
    q = rms_norm(q, qn_g)
    k = rms_norm(k, kn_g)
    inv_freq = ROPE_THETA ** (-jnp.arange(0, QK_ROPE, 2, dtype=jnp.float32) / QK_ROPE)
    ang = positions.astype(jnp.float32)[..., None] * inv_freq
    cos = jnp.cos(ang)[:, :, None, :]
    sin = jnp.sin(ang)[:, :, None, :]
    q = jnp.concatenate([q[..., :QK_NOPE], rope(q[..., QK_NOPE:], cos, sin)], axis=-1)
    k = jnp.concatenate([k[..., :QK_NOPE], rope(k[..., QK_NOPE:], cos, sin)], axis=-1)
    o = block_causal_attention(q, k, v).reshape(B, S, C_WIDTH) * jax.nn.silu(z)
    return x + o @ w_out


def setup_inputs(seed: int = 0) -> dict:
    key = jax.random.key(seed)
    ks = list(jax.random.split(key, 32))
    f32 = jnp.float32

    def normal(shape, scale):
        return jax.random.normal(ks.pop(), shape, f32) * scale

    def gain(shape):
        return 1.0 + normal(shape, 0.02)

    x = normal((BATCH, SEQ, D_MODEL), 1.0)
    offsets = jax.random.randint(ks.pop(), (BATCH, 1), 0, 64) * CHUNK
    positions = (offsets + jnp.arange(SEQ)[None, :]).astype(jnp.int32)
    a_b_if = jnp.concatenate([normal((N_A, M_HEADS), 0.1),
                              jnp.linspace(3.0, 6.0, M_HEADS)[None, :] + normal((N_A, M_HEADS), 0.1)], axis=-1)
    return {
        'x': x,
        'positions': positions,
        'a_norm_g': gain((N_A, D_MODEL)),
        'a_w_in': normal((N_A, D_MODEL, A_IN), D_MODEL ** -0.5),
        'a_b_if': a_b_if,
        'a_conv_w': normal((N_A, CONV_K, M_WIDTH), CONV_K ** -0.5),
        'a_conv_b': normal((N_A, M_WIDTH), 0.02),
        'a_w_q': normal((N_A, M_HEADS, M_HEAD_DIM, M_HEAD_DIM), M_HEAD_DIM ** -0.5),
        'a_w_k': normal((N_A, M_HEADS, M_HEAD_DIM, M_HEAD_DIM), M_HEAD_DIM ** -0.5),
        'a_w_v': normal((N_A, M_HEADS, M_HEAD_DIM, M_HEAD_DIM), M_HEAD_DIM ** -0.5),
        'a_head_norm_g': gain((N_A, M_WIDTH)),
        'a_skip': gain((N_A, M_WIDTH)),
        'a_pool_w': normal((N_A, P_GROUPS, P_GROUP_DIM, P_GROUP_DIM), P_GROUP_DIM ** -0.5),
        'a_pool_scale': 1.0 + normal((N_A, P_WIDTH), 0.1),
        'a_w_out': normal((N_A, A_OUT_IN, D_MODEL), A_OUT_IN ** -0.5),
        'c_norm_g': gain((N_C, D_MODEL)),
        'c_w_in': normal((N_C, D_MODEL, C_IN), D_MODEL ** -0.5),
        'c_q_norm_g': gain((N_C, Q_LORA)),
        'c_kv_norm_g': gain((N_C, KV_LORA)),
        'c_w_uq': normal((N_C, Q_LORA, C_HEADS * QK_HEAD), Q_LORA ** -0.5),
        'c_w_ukv': normal((N_C, KV_LORA, C_HEADS * (QK_NOPE + V_HEAD)), KV_LORA ** -0.5),
        'c_qn_g': gain((N_C, QK_HEAD)),
        'c_kn_g': gain((N_C, QK_HEAD)),
        'c_w_out': normal((N_C, C_WIDTH, D_MODEL), C_WIDTH ** -0.5),
    }


def reference(x, positions, a_norm_g, a_w_in, a_b_if, a_conv_w, a_conv_b, a_w_q, a_w_k, a_w_v,
              a_head_norm_g, a_skip, a_pool_w, a_pool_scale, a_w_out, c_norm_g, c_w_in,
              c_q_norm_g, c_kv_norm_g, c_w_uq, c_w_ukv, c_qn_g, c_kn_g, c_w_out):
    for layer in range(DEPTH):
        j = layer // 2
        if layer % 2 == 0:
            x = mlstm_pool_layer(x, a_norm_g[j], a_w_in[j], a_b_if[j], a_conv_w[j], a_conv_b[j],
                                 a_w_q[j], a_w_k[j], a_w_v[j], a_head_norm_g[j], a_skip[j],
                                 a_pool_w[j], a_pool_scale[j], a_w_out[j])
        else:
            x = mla_layer(x, positions, c_norm_g[j], c_w_in[j], c_q_norm_g[j], c_kv_norm_g[j],
                          c_w_uq[j], c_w_ukv[j], c_qn_g[j], c_kn_g[j], c_w_out[j])
    return x
```

```python
import functools

import jax
import jax.numpy as jnp
from jax import lax
from jax.experimental import pallas as pl
from jax.experimental.pallas import tpu as pltpu

F32 = jnp.float32
BF16 = jnp.bfloat16

D_MODEL = 1024
RMS_EPS = 1e-6
CHUNK = 64
M_HEADS = 4
M_HEAD_DIM = 256
CONV_K = 4
M_CHUNK = 256
P_WINDOWS = (2, 4, 8, 16)
P_GROUP_DIM = 256
HIST = 16
C_HEADS = 8
QK_NOPE = 128
QK_ROPE = 64
QK_HEAD = QK_NOPE + QK_ROPE
QK_PAD = 256
V_HEAD = 128
Q_LORA = 384
KV_LORA = 256
ROPE_THETA = 10000.0
LANES = 128

VMEM_LIMIT = 56 * 1024 * 1024


def _params(sem):
    return pltpu.CompilerParams(dimension_semantics=sem, vmem_limit_bytes=VMEM_LIMIT)


def _const_spec(shape):
    nd = len(shape)
    return pl.BlockSpec(shape, lambda *_: (0,) * nd, pipeline_mode=pl.Buffered(1))


def _rms(x, g):
    return x * lax.rsqrt(jnp.mean(x * x, axis=-1, keepdims=True) + RMS_EPS) * g


def _silu(x):
    return x * jax.nn.sigmoid(x)


def _log_sigmoid(x):
    return jnp.minimum(x, 0.0) - jnp.log1p(jnp.exp(-jnp.abs(x)))


def _dot(a, b):
    return jnp.dot(a, b, preferred_element_type=F32)


def _dot_nt(a, b):
    return lax.dot_general(a, b, (((1,), (1,)), ((), ())), preferred_element_type=F32)


def _dot_tn(a, b):
    return lax.dot_general(a, b, (((0,), (0,)), ((), ())), preferred_element_type=F32)


def _rope_table_kernel(pos_ref, invf_ref, cos_ref, sin_ref):
    ang = pos_ref[...] * invf_ref[...]
    cos_ref[...] = jnp.cos(ang)
    sin_ref[...] = jnp.sin(ang)


def _rope_tables(positions):
    n = positions.size
    half = QK_ROPE // 2
    per_row = LANES // half
    rows = n // per_row
    inv_freq = ROPE_THETA ** (-jnp.arange(0, QK_ROPE, 2, dtype=F32) / QK_ROPE)
    invf = jnp.tile(inv_freq, per_row)[None, :]
    pos = jnp.repeat(positions.reshape(-1).astype(F32), half).reshape(rows, LANES)
    tr = min(rows, 2048)
    cos, sin = pl.pallas_call(
        _rope_table_kernel,
        grid=(rows // tr,),
        in_specs=[pl.BlockSpec((tr, LANES), lambda i: (i, 0)), _const_spec((1, LANES))],
        out_specs=[pl.BlockSpec((tr, LANES), lambda i: (i, 0))] * 2,
        out_shape=[jax.ShapeDtypeStruct((rows, LANES), F32)] * 2,
        compiler_params=_params(("parallel",)),
        name="rope_tables",
    )(pos, invf)
    return cos.reshape(n, half), sin.reshape(n, half)


def _a_in_kernel(x_ref, g_ref, w5_ref, wg_ref, wgt_ref, bg_ref, bgt_ref,
                 xm_ref, so_ref, szm_ref, xp_ref, szp_ref, gate_ref, gatet_ref):
    yb = _rms(x_ref[...], g_ref[...]).astype(BF16)

    def proj(j):
        return _dot(yb, w5_ref[:, j * D_MODEL:(j + 1) * D_MODEL])

    xm_ref[...] = proj(0).astype(BF16)
    so_ref[...] = jax.nn.sigmoid(proj(1)).astype(BF16)
    szm_ref[...] = _silu(proj(2)).astype(BF16)
    xp_ref[...] = proj(3).astype(BF16)
    szp_ref[...] = _silu(proj(4)).astype(BF16)
    graw = _dot(yb, wg_ref[...]) + bg_ref[...]
    lane = lax.broadcasted_iota(jnp.int32, graw.shape, 1)
    gate_ref[...] = jnp.where(lane >= M_HEADS, _log_sigmoid(graw), graw)
    grawt = _dot_nt(wgt_ref[...], yb) + bgt_ref[...]
    row = lax.broadcasted_iota(jnp.int32, grawt.shape, 0)
    gatet_ref[...] = jnp.where(row >= M_HEADS, _log_sigmoid(grawt), grawt)


def _a_in(x2, norm_g, w5, wg, wgt, bg, bgt, tm):
    n = x2.shape[0]
    tok = lambda w: pl.BlockSpec((tm, w), lambda i: (i, 0))
    big = jax.ShapeDtypeStruct((n, D_MODEL), BF16)
    return pl.pallas_call(
        _a_in_kernel,
        grid=(n // tm,),
        in_specs=[tok(D_MODEL), _const_spec((1, D_MODEL)), _const_spec(w5.shape), _const_spec(wg.shape),
                  _const_spec(wgt.shape), _const_spec(bg.shape), _const_spec(bgt.shape)],
        out_specs=[tok(D_MODEL)] * 5 + [tok(LANES), pl.BlockSpec((2 * M_HEADS, tm), lambda i: (0, i))],
        out_shape=[big] * 5 + [jax.ShapeDtypeStruct((n, LANES), F32),
                               jax.ShapeDtypeStruct((2 * M_HEADS, n), F32)],
        compiler_params=_params(("parallel",)),
        name="a_in_proj",
    )(x2, norm_g, w5, wg, wgt, bg, bgt)


def _mlstm_kernel(xm_ref, xmh_ref, so_ref, szm_ref, g_ref, gt_ref, cw_ref, cb_ref,
                  wq_ref, wk_ref, wv_ref, hng_ref, skip_ref, ya_ref,
                  c_scr, n_scr, m_scr, xbuf):
    t = xm_ref.shape[0]
    si = pl.program_id(1)

    @pl.when(si == 0)
    def _():
        c_scr[...] = jnp.zeros_like(c_scr)
        n_scr[...] = jnp.zeros_like(n_scr)
        m_scr[...] = jnp.zeros_like(m_scr)

    hist = jnp.where(si == 0, 0.0, xmh_ref[...].astype(F32))
    xbuf[0:HIST, :] = hist
    xbuf[HIST:HIST + t, :] = xm_ref[...].astype(F32)
    conv = cb_ref[...]
    for j in range(CONV_K):
        conv = conv + cw_ref[j:j + 1, :] * xbuf[pl.ds(HIST - (CONV_K - 1) + j, t), :]
    xc = _silu(conv)

    ri = lax.broadcasted_iota(jnp.int32, (t, t), 0)
    ci = lax.broadcasted_iota(jnp.int32, (t, t), 1)
    tril = ci <= ri
    triu = ri <= ci
    gcol = g_ref[...]
    neg_inf = -jnp.inf

    for h in range(M_HEADS):
        hs = slice(h * M_HEAD_DIM, (h + 1) * M_HEAD_DIM)
        xc_h = xc[:, hs]
        xcb = xc_h.astype(BF16)
        q = _dot(xcb, wq_ref[h])
        k = _dot(xcb, wk_ref[h])
        v = _dot(xm_ref[:, hs], wv_ref[h])
        qb, vb = q.astype(BF16), v.astype(BF16)

        i_row = gt_ref[h:h + 1, :]
        lf_row = gt_ref[M_HEADS + h:M_HEADS + h + 1, :]
        i_col = gcol[:, h:h + 1]
        lf_col = gcol[:, M_HEADS + h:M_HEADS + h + 1]
        b_col = jnp.sum(jnp.where(tril, lf_row, 0.0), axis=1, keepdims=True)
        b_row = jnp.sum(jnp.where(triu, lf_col, 0.0), axis=0, keepdims=True)
        a_row = i_row - b_row
        a_col = i_col - b_col
        m_old = m_scr[h][:, 0:1]
        big_m = jnp.maximum(jnp.max(jnp.where(tril, a_row, neg_inf), axis=1, keepdims=True), m_old)
        decay = jnp.exp(jnp.where(tril, a_row - big_m, neg_inf))
        s = _dot_nt(qb, k.astype(BF16)) * decay
        w_inter = jnp.exp(m_old - big_m)
        n_row = n_scr[h]
        den = (jnp.sum(s, axis=1, keepdims=True)
               + w_inter * jnp.sum(q * n_row, axis=1, keepdims=True))
        num = _dot(s.astype(BF16), vb) + w_inter * _dot(qb, c_scr[h].astype(BF16))
        hh = num * (1.0 / jnp.maximum(jnp.abs(den), jnp.exp(-(b_col + big_m))))

        m_last = jnp.maximum(jnp.max(a_row, axis=1, keepdims=True), m_old)
        b_last = jnp.sum(lf_row, axis=1, keepdims=True)
        kw = k * jnp.exp(a_col - m_last)
        wc = jnp.exp(m_old - m_last)
        c_scr[h] = wc * c_scr[h] + _dot_tn(kw.astype(BF16), vb)
        n_scr[h] = wc * n_row + jnp.sum(kw, axis=0, keepdims=True)
        m_scr[h] = jnp.broadcast_to(b_last + m_last, (1, LANES))

        hm = hh * so_ref[:, hs].astype(F32)
        hm = _rms(hm, hng_ref[:, hs])
        ya = (hm + skip_ref[:, hs] * xc_h) * szm_ref[:, hs].astype(F32)
        ya_ref[:, hs] = ya.astype(BF16)


def _mlstm(xm, so, szm, gates, gates_t, conv_w, conv_b, wq, wk, wv, hng, skip, batch, seq):
    n = xm.shape[0]
    t = M_CHUNK
    spt = seq // t
    tok = lambda w: pl.BlockSpec((t, w), lambda b, s: (b * spt + s, 0))
    hist = pl.BlockSpec((HIST, D_MODEL),
                        lambda b, s: (jnp.maximum((b * spt + s) * (t // HIST) - 1, 0), 0))
    return pl.pallas_call(
        _mlstm_kernel,
        grid=(batch, spt),
        in_specs=[tok(D_MODEL), hist, tok(D_MODEL), tok(D_MODEL), tok(LANES),
                  pl.BlockSpec((2 * M_HEADS, t), lambda b, s: (0, b * spt + s)),
                  _const_spec(conv_w.shape), _const_spec(conv_b.shape),
                  _const_spec(wq.shape), _const_spec(wk.shape), _const_spec(wv.shape),
                  _const_spec(hng.shape), _const_spec(skip.shape)],
        out_specs=tok(D_MODEL),
        out_shape=jax.ShapeDtypeStruct((n, D_MODEL), BF16),
        scratch_shapes=[pltpu.VMEM((M_HEADS, M_HEAD_DIM, M_HEAD_DIM), F32),
                        pltpu.VMEM((M_HEADS, 1, M_HEAD_DIM), F32),
                        pltpu.VMEM((M_HEADS, 1, LANES), F32),
                        pltpu.VMEM((HIST + t, D_MODEL), F32)],
        compiler_params=_params(("parallel", "arbitrary")),
        name="mlstm",
    )(xm, xm, so, szm, gates, gates_t, conv_w, conv_b, wq, wk, wv, hng, skip)


def _a_out_kernel(x_ref, xp_ref, xph_ref, szp_ref, ya_ref, pw_ref, ps_ref, wo_ref, o_ref, xbuf,
                  *, tiles_per_seq):
    tm = x_ref.shape[0]
    ti = pl.program_id(0) % tiles_per_seq
    xbuf[0:HIST, :] = jnp.where(ti == 0, 0.0, xph_ref[...].astype(F32))
    xbuf[HIST:HIST + tm, :] = xp_ref[...].astype(F32)
    tpos = ti * tm + lax.broadcasted_iota(jnp.int32, (tm, 1), 0)
    acc = x_ref[...] + _dot(ya_ref[...], wo_ref[0:D_MODEL, :])
    for g, w in enumerate(P_WINDOWS):
        gs = slice(g * P_GROUP_DIM, (g + 1) * P_GROUP_DIM)
        xe = xbuf[:, gs]
        tot = xe
        sh = 1
        while sh < w:
            tot = tot + pltpu.roll(tot, sh, axis=0)
            sh *= 2
        inv_cnt = 1.0 / jnp.minimum(tpos + 1, w).astype(F32)
        mix = tot[HIST:, :] * inv_cnt - xe[HIST:, :]
        yb = _dot(mix.astype(BF16), pw_ref[g]) * ps_ref[:, gs] * szp_ref[:, gs].astype(F32)
        acc = acc + _dot(yb.astype(BF16), wo_ref[D_MODEL + g * P_GROUP_DIM:D_MODEL + (g + 1) * P_GROUP_DIM, :])
    o_ref[...] = acc


def _a_out(x2, xp, szp, ya, pool_w, pool_scale, w_out, seq, tm):
    n = x2.shape[0]
    tok = lambda w: pl.BlockSpec((tm, w), lambda i: (i, 0))
    hist = pl.BlockSpec((HIST, D_MODEL), lambda i: (jnp.maximum(i * (tm // HIST) - 1, 0), 0))
    return pl.pallas_call(
        functools.partial(_a_out_kernel, tiles_per_seq=seq // tm),
        grid=(n // tm,),
        in_specs=[tok(D_MODEL), tok(D_MODEL), hist, tok(D_MODEL), tok(D_MODEL),
                  _const_spec(pool_w.shape), _const_spec(pool_scale.shape), _const_spec(w_out.shape)],
        out_specs=tok(D_MODEL),
        out_shape=jax.ShapeDtypeStruct((n, D_MODEL), F32),
        scratch_shapes=[pltpu.VMEM((HIST + tm, D_MODEL), F32)],
        compiler_params=_params(("parallel",)),
        name="a_pool_out_proj",
    )(x2, xp, xp, szp, ya, pool_w, pool_scale, w_out)


def _layer_a(x2, p, batch, seq):
    xm, so, szm, xp, szp, gates, gates_t = _a_in(x2, p["norm_g"], p["w5"], p["wg"], p["wgt"], p["bg"],
                                                 p["bgt"], tm=512)
    ya = _mlstm(xm, so, szm, gates, gates_t, p["conv_w"], p["conv_b"], p["wq"], p["wk"], p["wv"],
                p["hng"], p["skip"], batch, seq)
    return _a_out(x2, xp, szp, ya, p["pool_w"], p["pool_scale"], p["w_out"], seq, tm=512)


def _prep_a(norm_g, w_in, b_if, conv_w, conv_b, w_q, w_k, w_v, head_norm_g, skip, pool_w, pool_scale, w_out):
    mw = M_HEADS * M_HEAD_DIM
    g0 = 2 * mw
    g1 = g0 + 2 * M_HEADS
    w5 = jnp.concatenate([w_in[:, :g0], w_in[:, g1:]], axis=1).astype(BF16)
    wg = w_in[:, g0:g1]
    pad = LANES - 2 * M_HEADS
    return dict(
        norm_g=norm_g[None, :], w5=w5,
        wg=jnp.pad(wg, ((0, 0), (0, pad))).astype(BF16), wgt=wg.T.astype(BF16),
        bg=jnp.pad(b_if, (0, pad))[None, :], bgt=b_if[:, None],
        conv_w=conv_w, conv_b=conv_b[None, :],
        wq=w_q.astype(BF16), wk=(w_k * (M_HEAD_DIM ** -0.5)).astype(BF16), wv=w_v.astype(BF16),
        hng=head_norm_g[None, :], skip=skip[None, :],
        pool_w=pool_w.astype(BF16), pool_scale=pool_scale[None, :], w_out=w_out.astype(BF16))


def _c_in_kernel(x_ref, g_ref, wcq_ref, wckv_ref, wkpe_ref, wz_ref, qng_ref, kvng_ref,
                 wuq_ref, wuk_ref, wuv_ref, qg_ref, kg_ref, cs_ref, sn_ref,
                 q_ref, k_ref, v_ref, sz_ref):
    yb = _rms(x_ref[...], g_ref[...]).astype(BF16)
    sz_ref[...] = _silu(_dot(yb, wz_ref[...])).astype(BF16)
    cqn = _rms(_dot(yb, wcq_ref[...]), qng_ref[...]).astype(BF16)
    ckvn = _rms(_dot(yb, wckv_ref[...]), kvng_ref[...]).astype(BF16)
    kpe = _dot(yb, wkpe_ref[...])
    v_ref[...] = _dot(ckvn, wuv_ref[...]).astype(BF16)
    q = _dot(cqn, wuq_ref[...])
    kn = _dot(ckvn, wuk_ref[...])
    cs, sn = cs_ref[...], sn_ref[...]

    def rope(u):
        return u * cs + pltpu.roll(u, LANES // 2, axis=1) * sn

    qg_n, qg_r = qg_ref[:, 0:QK_NOPE], qg_ref[:, QK_NOPE:QK_PAD]
    kg_n, kg_r = kg_ref[:, 0:QK_NOPE], kg_ref[:, QK_NOPE:QK_PAD]
    kpe_ss = jnp.sum(kpe * kpe, axis=1, keepdims=True)
    kr = rope(kpe * kg_r)
    scale = QK_HEAD ** -0.5
    for h in range(C_HEADS):
        o = h * QK_PAD
        qn = q[:, o:o + QK_NOPE]
        qr = q[:, o + QK_NOPE:o + QK_PAD]
        r = lax.rsqrt(jnp.sum(qn * qn + qr * qr, axis=1, keepdims=True) * (1.0 / QK_HEAD) + RMS_EPS) * scale
        q_ref[:, o:o + QK_NOPE] = (qn * r * qg_n).astype(BF16)
        q_ref[:, o + QK_NOPE:o + QK_PAD] = rope(qr * r * qg_r).astype(BF16)
        knh = kn[:, h * QK_NOPE:(h + 1) * QK_NOPE]
        rk = lax.rsqrt((jnp.sum(knh * knh, axis=1, keepdims=True) + kpe_ss) * (1.0 / QK_HEAD) + RMS_EPS)
        k_ref[:, o:o + QK_NOPE] = (knh * rk * kg_n).astype(BF16)
        k_ref[:, o + QK_NOPE:o + QK_PAD] = (kr * rk).astype(BF16)


def _c_in(x2, p, cs, sn, tm):
    n = x2.shape[0]
    tok = lambda w: pl.BlockSpec((tm, w), lambda i: (i, 0))
    names = ("norm_g", "wcq", "wckv", "wkpe", "wz", "qng", "kvng", "wuq", "wuk", "wuv", "qg", "kg")
    return pl.pallas_call(
        _c_in_kernel,
        grid=(n // tm,),
        in_specs=[tok(D_MODEL)] + [_const_spec(p[k].shape) for k in names] + [tok(LANES), tok(LANES)],
        out_specs=[tok(C_HEADS * QK_PAD), tok(C_HEADS * QK_PAD), tok(C_HEADS * V_HEAD), tok(D_MODEL)],
        out_shape=[jax.ShapeDtypeStruct((n, C_HEADS * QK_PAD), BF16),
                   jax.ShapeDtypeStruct((n, C_HEADS * QK_PAD), BF16),
                   jax.ShapeDtypeStruct((n, C_HEADS * V_HEAD), BF16),
                   jax.ShapeDtypeStruct((n, D_MODEL), BF16)],
        compiler_params=_params(("parallel",)),
        name="c_in_proj",
    )(x2, *[p[k] for k in names], cs, sn)


def _attn_kernel(q_ref, k_ref, v_ref, o_ref):
    tq = q_ref.shape[0]
    tk = tq
    qi = pl.program_id(2)
    q = q_ref[...]

    def step(kj, vj, carry, mask):
        m, l, acc = carry
        s = _dot_nt(q, kj)
        if mask is not None:
            s = jnp.where(mask, s, -jnp.inf)
        m_new = jnp.maximum(m, jnp.max(s, axis=1, keepdims=True))
        alpha = jnp.exp(m - m_new)
        p = jnp.exp(s - m_new)
        l = alpha * l + jnp.sum(p, axis=1, keepdims=True)
        acc = alpha * acc + _dot(p.astype(BF16), vj)
        return m_new, l, acc

    def body(j, carry):
        off = pl.multiple_of(j * tk, tk)
        return step(k_ref[pl.ds(off, tk), :], v_ref[pl.ds(off, tk), :], carry, None)

    init = (jnp.full((tq, 1), -jnp.inf, F32), jnp.zeros((tq, 1), F32), jnp.zeros((tq, V_HEAD), F32))
    carry = lax.fori_loop(0, qi, body, init)
    rc = lax.broadcasted_iota(jnp.int32, (tq, tk), 0) // CHUNK
    cc = lax.broadcasted_iota(jnp.int32, (tq, tk), 1) // CHUNK
    off = pl.multiple_of(qi * tk, tk)
    _, l, acc = step(k_ref[pl.ds(off, tk), :], v_ref[pl.ds(off, tk), :], carry, cc <= rc)
    o_ref[...] = (acc * (1.0 / l)).astype(BF16)


def _attention(q, k, v, batch, seq, tq):
    n = q.shape[0]
    nq = seq // tq
    return pl.pallas_call(
        _attn_kernel,
        grid=(batch, C_HEADS, nq),
        in_specs=[pl.BlockSpec((tq, QK_PAD), lambda b, h, i: (b * nq + i, h)),
                  pl.BlockSpec((seq, QK_PAD), lambda b, h, i: (b, h)),
                  pl.BlockSpec((seq, V_HEAD), lambda b, h, i: (b, h))],
        out_specs=pl.BlockSpec((tq, V_HEAD), lambda b, h, i: (b * nq + i, h)),
        out_shape=jax.ShapeDtypeStruct((n, C_HEADS * V_HEAD), BF16),
        compiler_params=_params(("parallel", "parallel", "arbitrary")),
        name="block_causal_attention",
    )(q, k, v)


def _c_out_kernel(x_ref, o_ref, sz_ref, wo_ref, out_ref):
    gated = (o_ref[...].astype(F32) * sz_ref[...].astype(F32)).astype(BF16)
    out_ref[...] = x_ref[...] + _dot(gated, wo_ref[...])


def _c_out(x2, o, sz, w_out, tm):
    n = x2.shape[0]
    tok = lambda w: pl.BlockSpec((tm, w), lambda i: (i, 0))
    return pl.pallas_call(
        _c_out_kernel,
        grid=(n // tm,),
        in_specs=[tok(D_MODEL), tok(D_MODEL), tok(D_MODEL), _const_spec(w_out.shape)],
        out_specs=tok(D_MODEL),
        out_shape=jax.ShapeDtypeStruct((n, D_MODEL), F32),
        compiler_params=_params(("parallel",)),
        name="c_out_proj",
    )(x2, o, sz, w_out)


def _layer_c(x2, p, cs, sn, batch, seq):
    q, k, v, sz = _c_in(x2, p, cs, sn, tm=256)
    o = _attention(q, k, v, batch, seq, tq=256)
    return _c_out(x2, o, sz, p["w_out"], tm=512)


def _pad_rope_cols(a):
    half = QK_ROPE // 2
    z = jnp.zeros(a.shape[:-1] + (half,), a.dtype)
    return jnp.concatenate([a[..., :half], z, a[..., half:], z], axis=-1)


def _pad_head_cols(a):
    return jnp.concatenate([a[..., :QK_NOPE], _pad_rope_cols(a[..., QK_NOPE:])], axis=-1)


def _prep_c(norm_g, w_in, q_norm_g, kv_norm_g, w_uq, w_ukv, qn_g, kn_g, w_out):
    c0, c1, c2 = Q_LORA, Q_LORA + KV_LORA, Q_LORA + KV_LORA + QK_ROPE
    wuq = _pad_head_cols(w_uq.reshape(Q_LORA, C_HEADS, QK_HEAD)).reshape(Q_LORA, C_HEADS * QK_PAD)
    wukv = w_ukv.reshape(KV_LORA, C_HEADS, QK_NOPE + V_HEAD)
    return dict(
        norm_g=norm_g[None, :],
        wcq=w_in[:, :c0].astype(BF16), wckv=w_in[:, c0:c1].astype(BF16),
        wkpe=_pad_rope_cols(w_in[:, c1:c2]).astype(BF16), wz=w_in[:, c2:].astype(BF16),
        qng=q_norm_g[None, :], kvng=kv_norm_g[None, :],
        wuq=wuq.astype(BF16),
        wuk=wukv[:, :, :QK_NOPE].reshape(KV_LORA, C_HEADS * QK_NOPE).astype(BF16),
        wuv=wukv[:, :, QK_NOPE:].reshape(KV_LORA, C_HEADS * V_HEAD).astype(BF16),
        qg=_pad_head_cols(qn_g)[None, :], kg=_pad_head_cols(kn_g)[None, :],
        w_out=w_out.astype(BF16))


def kernel(x, positions, a_norm_g, a_w_in, a_b_if, a_conv_w, a_conv_b, a_w_q, a_w_k, a_w_v, a_head_norm_g, a_skip, a_pool_w, a_pool_scale, a_w_out, c_norm_g, c_w_in, c_q_norm_g, c_kv_norm_g, c_w_uq, c_w_ukv, c_qn_g, c_kn_g, c_w_out):
    batch, seq, d = x.shape
    depth = a_norm_g.shape[0] + c_norm_g.shape[0]
    a_args = (a_norm_g, a_w_in, a_b_if, a_conv_w, a_conv_b, a_w_q, a_w_k, a_w_v, a_head_norm_g, a_skip,
              a_pool_w, a_pool_scale, a_w_out)
    c_args = (c_norm_g, c_w_in, c_q_norm_g, c_kv_norm_g, c_w_uq, c_w_ukv, c_qn_g, c_kn_g, c_w_out)
    cos, sin = _rope_tables(positions)
    zeros = jnp.zeros_like(cos)
    cs = jnp.concatenate([cos, zeros, cos, zeros], axis=-1)
    sn = jnp.concatenate([-sin, zeros, sin, zeros], axis=-1)
    x2 = x.reshape(batch * seq, d)
    for layer in range(depth):
        j = layer // 2
        if layer % 2 == 0:
            x2 = _layer_a(x2, _prep_a(*[a[j] for a in a_args]), batch, seq)
        else:
            x2 = _layer_c(x2, _prep_c(*[a[j] for a in c_args]), cs, sn, batch, seq)
    return x2.reshape(batch, seq, d)
```

```python
import functools

import jax
import jax.numpy as jnp
from jax import lax
from jax.experimental import pallas as pl
from jax.experimental.pallas import tpu as pltpu

F32 = jnp.float32
BF16 = jnp.bfloat16

D_MODEL = 1024
RMS_EPS = 1e-6
CHUNK = 64
M_HEADS = 4
M_HEAD_DIM = 256
CONV_K = 4
M_CHUNK = 256
P_WINDOWS = (2, 4, 8, 16)
P_GROUP_DIM = 256
HIST = 16
C_HEADS = 8
QK_NOPE = 128
QK_ROPE = 64
QK_HEAD = QK_NOPE + QK_ROPE
QK_PAD = 256
V_HEAD = 128
Q_LORA = 384
KV_LORA = 256
ROPE_THETA = 10000.0
LANES = 128

VMEM_LIMIT = 56 * 1024 * 1024


def _params(sem):
    return pltpu.CompilerParams(dimension_semantics=sem, vmem_limit_bytes=VMEM_LIMIT)


def _const_spec(shape):
    nd = len(shape)
    return pl.BlockSpec(shape, lambda *_: (0,) * nd, pipeline_mode=pl.Buffered(1))


def _rms(x, g):
    return x * lax.rsqrt(jnp.mean(x * x, axis=-1, keepdims=True) + RMS_EPS) * g


def _silu(x):
    return x * jax.nn.sigmoid(x)


def _log_sigmoid(x):
    return jnp.minimum(x, 0.0) - jnp.log1p(jnp.exp(-jnp.abs(x)))


def _dot(a, b):
    return jnp.dot(a, b, preferred_element_type=F32)


def _dot_nt(a, b):
    return lax.dot_general(a, b, (((1,), (1,)), ((), ())), preferred_element_type=F32)


def _dot_tn(a, b):
    return lax.dot_general(a, b, (((0,), (0,)), ((), ())), preferred_element_type=F32)


def _rope_table_kernel(pos_ref, invf_ref, cos_ref, sin_ref):
    ang = pos_ref[...] * invf_ref[...]
    cos_ref[...] = jnp.cos(ang)
    sin_ref[...] = jnp.sin(ang)


def _rope_tables(positions):
    n = positions.size
    half = QK_ROPE // 2
    per_row = LANES // half
    rows = n // per_row
    inv_freq = ROPE_THETA ** (-jnp.arange(0, QK_ROPE, 2, dtype=F32) / QK_ROPE)
    invf = jnp.tile(inv_freq, per_row)[None, :]
    pos = jnp.repeat(positions.reshape(-1).astype(F32), half).reshape(rows, LANES)
    tr = min(rows, 2048)
    cos, sin = pl.pallas_call(
        _rope_table_kernel,
        grid=(rows // tr,),
        in_specs=[pl.BlockSpec((tr, LANES), lambda i: (i, 0)), _const_spec((1, LANES))],
        out_specs=[pl.BlockSpec((tr, LANES), lambda i: (i, 0))] * 2,
        out_shape=[jax.ShapeDtypeStruct((rows, LANES), F32)] * 2,
        compiler_params=_params(("parallel",)),
        name="rope_tables",
    )(pos, invf)
    return cos.reshape(n, half), sin.reshape(n, half)


def _a_in_kernel(x_ref, g_ref, w5_ref, wg_ref, wgt_ref, bg_ref, bgt_ref,
                 xm_ref, so_ref, szm_ref, xp_ref, szp_ref, gate_ref, gatet_ref):
    yb = _rms(x_ref[...], g_ref[...]).astype(BF16)

    def proj(j):
        return _dot(yb, w5_ref[:, j * D_MODEL:(j + 1) * D_MODEL])

    xm_ref[...] = proj(0).astype(BF16)
    so_ref[...] = jax.nn.sigmoid(proj(1)).astype(BF16)
    szm_ref[...] = _silu(proj(2)).astype(BF16)
    xp_ref[...] = proj(3).astype(BF16)
    szp_ref[...] = _silu(proj(4)).astype(BF16)
    graw = _dot(yb, wg_ref[...]) + bg_ref[...]
    lane = lax.broadcasted_iota(jnp.int32, graw.shape, 1)
    gate_ref[...] = jnp.where(lane >= M_HEADS, _log_sigmoid(graw), graw)
    grawt = _dot_nt(wgt_ref[...], yb) + bgt_ref[...]
    row = lax.broadcasted_iota(jnp.int32, grawt.shape, 0)
    gatet_ref[...] = jnp.where(row >= M_HEADS, _log_sigmoid(grawt), grawt)


def _a_in(x2, norm_g, w5, wg, wgt, bg, bgt, tm):
    n = x2.shape[0]
    tok = lambda w: pl.BlockSpec((tm, w), lambda i: (i, 0))
    big = jax.ShapeDtypeStruct((n, D_MODEL), BF16)
    return pl.pallas_call(
        _a_in_kernel,
        grid=(n // tm,),
        in_specs=[tok(D_MODEL), _const_spec((1, D_MODEL)), _const_spec(w5.shape), _const_spec(wg.shape),
                  _const_spec(wgt.shape), _const_spec(bg.shape), _const_spec(bgt.shape)],
        out_specs=[tok(D_MODEL)] * 5 + [tok(LANES), pl.BlockSpec((2 * M_HEADS, tm), lambda i: (0, i))],
        out_shape=[big] * 5 + [jax.ShapeDtypeStruct((n, LANES), F32),
                               jax.ShapeDtypeStruct((2 * M_HEADS, n), F32)],
        compiler_params=_params(("parallel",)),
        name="a_in_proj",
    )(x2, norm_g, w5, wg, wgt, bg, bgt)


def _mlstm_kernel(xm_ref, xmh_ref, so_ref, szm_ref, g_ref, gt_ref, cw_ref, cb_ref,
                  wq_ref, wk_ref, wv_ref, hng_ref, skip_ref, ya_ref,
                  c_scr, n_scr, m_scr, xbuf):
    t = xm_ref.shape[0]
    si = pl.program_id(1)

    @pl.when(si == 0)
    def _():
        c_scr[...] = jnp.zeros_like(c_scr)
        n_scr[...] = jnp.zeros_like(n_scr)
        m_scr[...] = jnp.zeros_like(m_scr)

    hist = jnp.where(si == 0, 0.0, xmh_ref[...].astype(F32))
    xbuf[0:HIST, :] = hist
    xbuf[HIST:HIST + t, :] = xm_ref[...].astype(F32)
    conv = cb_ref[...]
    for j in range(CONV_K):
        conv = conv + cw_ref[j:j + 1, :] * xbuf[pl.ds(HIST - (CONV_K - 1) + j, t), :]
    xc = _silu(conv)

    ri = lax.broadcasted_iota(jnp.int32, (t, t), 0)
    ci = lax.broadcasted_iota(jnp.int32, (t, t), 1)
    tril = ci <= ri
    triu = ri <= ci
    gcol = g_ref[...]
    neg_inf = -jnp.inf

    for h in range(M_HEADS):
        hs = slice(h * M_HEAD_DIM, (h + 1) * M_HEAD_DIM)
        xc_h = xc[:, hs]
        xcb = xc_h.astype(BF16)
        q = _dot(xcb, wq_ref[h])
        k = _dot(xcb, wk_ref[h])
        v = _dot(xm_ref[:, hs], wv_ref[h])
        qb, vb = q.astype(BF16), v.astype(BF16)

        i_row = gt_ref[h:h + 1, :]
        lf_row = gt_ref[M_HEADS + h:M_HEADS + h + 1, :]
        i_col = gcol[:, h:h + 1]
        lf_col = gcol[:, M_HEADS + h:M_HEADS + h + 1]
        b_col = jnp.sum(jnp.where(tril, lf_row, 0.0), axis=1, keepdims=True)
        b_row = jnp.sum(jnp.where(triu, lf_col, 0.0), axis=0, keepdims=True)
        a_row = i_row - b_row
        a_col = i_col - b_col
        m_old = m_scr[h][:, 0:1]
        big_m = jnp.maximum(jnp.max(jnp.where(tril, a_row, neg_inf), axis=1, keepdims=True), m_old)
        decay = jnp.exp(jnp.where(tril, a_row - big_m, neg_inf))
        s = _dot_nt(qb, k.astype(BF16)) * decay
        w_inter = jnp.exp(m_old - big_m)
        n_row = n_scr[h]
        den = (jnp.sum(s, axis=1, keepdims=True)
               + w_inter * jnp.sum(q * n_row, axis=1, keepdims=True))
        num = _dot(s.astype(BF16), vb) + w_inter * _dot(qb, c_scr[h].astype(BF16))
        hh = num * (1.0 / jnp.maximum(jnp.abs(den), jnp.exp(-(b_col + big_m))))

        m_last = jnp.maximum(jnp.max(a_row, axis=1, keepdims=True), m_old)
        b_last = jnp.sum(lf_row, axis=1, keepdims=True)
        kw = k * jnp.exp(a_col - m_last)
        wc = jnp.exp(m_old - m_last)
        c_scr[h] = wc * c_scr[h] + _dot_tn(kw.astype(BF16), vb)
        n_scr[h] = wc * n_row + jnp.sum(kw, axis=0, keepdims=True)
        m_scr[h] = jnp.broadcast_to(b_last + m_last, (1, LANES))

        hm = hh * so_ref[:, hs].astype(F32)
        hm = _rms(hm, hng_ref[:, hs])
        ya = (hm + skip_ref[:, hs] * xc_h) * szm_ref[:, hs].astype(F32)
        ya_ref[:, hs] = ya.astype(BF16)


def _mlstm(xm, so, szm, gates, gates_t, conv_w, conv_b, wq, wk, wv, hng, skip, batch, seq):
    n = xm.shape[0]
    t = M_CHUNK
    spt = seq // t
    tok = lambda w: pl.BlockSpec((t, w), lambda b, s: (b * spt + s, 0))
    hist = pl.BlockSpec((HIST, D_MODEL),
                        lambda b, s: (jnp.maximum((b * spt + s) * (t // HIST) - 1, 0), 0))
    return pl.pallas_call(
        _mlstm_kernel,
        grid=(batch, spt),
        in_specs=[tok(D_MODEL), hist, tok(D_MODEL), tok(D_MODEL), tok(LANES),
                  pl.BlockSpec((2 * M_HEADS, t), lambda b, s: (0, b * spt + s)),
                  _const_spec(conv_w.shape), _const_spec(conv_b.shape),
                  _const_spec(wq.shape), _const_spec(wk.shape), _const_spec(wv.shape),
                  _const_spec(hng.shape), _const_spec(skip.shape)],
        out_specs=tok(D_MODEL),
        out_shape=jax.ShapeDtypeStruct((n, D_MODEL), BF16),
        scratch_shapes=[pltpu.VMEM((M_HEADS, M_HEAD_DIM, M_HEAD_DIM), F32),
                        pltpu.VMEM((M_HEADS, 1, M_HEAD_DIM), F32),
                        pltpu.VMEM((M_HEADS, 1, LANES), F32),
                        pltpu.VMEM((HIST + t, D_MODEL), F32)],
        compiler_params=_params(("parallel", "arbitrary")),
        name="mlstm",
    )(xm, xm, so, szm, gates, gates_t, conv_w, conv_b, wq, wk, wv, hng, skip)


def _a_out_kernel(x_ref, xp_ref, xph_ref, szp_ref, ya_ref, pw_ref, ps_ref, wo_ref, o_ref, xbuf,
                  *, tiles_per_seq):
    tm = x_ref.shape[0]
    ti = pl.program_id(0) % tiles_per_seq
    xbuf[0:HIST, :] = jnp.where(ti == 0, 0.0, xph_ref[...].astype(F32))
    xbuf[HIST:HIST + tm, :] = xp_ref[...].astype(F32)
    tpos = ti * tm + lax.broadcasted_iota(jnp.int32, (tm, 1), 0)
    acc = x_ref[...] + _dot(ya_ref[...], wo_ref[0:D_MODEL, :])
    for g, w in enumerate(P_WINDOWS):
        gs = slice(g * P_GROUP_DIM, (g + 1) * P_GROUP_DIM)
        xe = xbuf[:, gs]
        tot = xe
        sh = 1
        while sh < w:
            tot = tot + pltpu.roll(tot, sh, axis=0)
            sh *= 2
        inv_cnt = 1.0 / jnp.minimum(tpos + 1, w).astype(F32)
        mix = tot[HIST:, :] * inv_cnt - xe[HIST:, :]
        yb = _dot(mix.astype(BF16), pw_ref[g]) * ps_ref[:, gs] * szp_ref[:, gs].astype(F32)
        acc = acc + _dot(yb.astype(BF16), wo_ref[D_MODEL + g * P_GROUP_DIM:D_MODEL + (g + 1) * P_GROUP_DIM, :])
    o_ref[...] = acc


def _a_out(x2, xp, szp, ya, pool_w, pool_scale, w_out, seq, tm):
    n = x2.shape[0]
    tok = lambda w: pl.BlockSpec((tm, w), lambda i: (i, 0))
    hist = pl.BlockSpec((HIST, D_MODEL), lambda i: (jnp.maximum(i * (tm // HIST) - 1, 0), 0))
    return pl.pallas_call(
        functools.partial(_a_out_kernel, tiles_per_seq=seq // tm),
        grid=(n // tm,),
        in_specs=[tok(D_MODEL), tok(D_MODEL), hist, tok(D_MODEL), tok(D_MODEL),
                  _const_spec(pool_w.shape), _const_spec(pool_scale.shape), _const_spec(w_out.shape)],
        out_specs=tok(D_MODEL),
        out_shape=jax.ShapeDtypeStruct((n, D_MODEL), F32),
        scratch_shapes=[pltpu.VMEM((HIST + tm, D_MODEL), F32)],
        compiler_params=_params(("parallel",)),
        name="a_pool_out_proj",
    )(x2, xp, xp, szp, ya, pool_w, pool_scale, w_out)


def _layer_a(x2, p, batch, seq):
    xm, so, szm, xp, szp, gates, gates_t = _a_in(x2, p["norm_g"], p["w5"], p["wg"], p["wgt"], p["bg"],
                                                 p["bgt"], tm=512)
    ya = _mlstm(xm, so, szm, gates, gates_t, p["conv_w"], p["conv_b"], p["wq"], p["wk"], p["wv"],
                p["hng"], p["skip"], batch, seq)
    return _a_out(x2, xp, szp, ya, p["pool_w"], p["pool_scale"], p["w_out"], seq, tm=512)


def _prep_a(norm_g, w_in, b_if, conv_w, conv_b, w_q, w_k, w_v, head_norm_g, skip, pool_w, pool_scale, w_out):
    mw = M_HEADS * M_HEAD_DIM
    g0 = 2 * mw
    g1 = g0 + 2 * M_HEADS
    w5 = jnp.concatenate([w_in[:, :g0], w_in[:, g1:]], axis=1).astype(BF16)
    wg = w_in[:, g0:g1]
    pad = LANES - 2 * M_HEADS
    return dict(
        norm_g=norm_g[None, :], w5=w5,
        wg=jnp.pad(wg, ((0, 0), (0, pad))).astype(BF16), wgt=wg.T.astype(BF16),
        bg=jnp.pad(b_if, (0, pad))[None, :], bgt=b_if[:, None],
        conv_w=conv_w, conv_b=conv_b[None, :],
        wq=w_q.astype(BF16), wk=(w_k * (M_HEAD_DIM ** -0.5)).astype(BF16), wv=w_v.astype(BF16),
        hng=head_norm_g[None, :], skip=skip[None, :],
        pool_w=pool_w.astype(BF16), pool_scale=pool_scale[None, :], w_out=w_out.astype(BF16))


def _c_in_kernel(x_ref, g_ref, wcq_ref, wckv_ref, wkpe_ref, wz_ref, qng_ref, kvng_ref,
                 wuq_ref, wuk_ref, wuv_ref, qg_ref, kg_ref, cs_ref, sn_ref,
                 q_ref, k_ref, v_ref, sz_ref, *, sub):
    qg_n, qg_r = qg_ref[:, 0:QK_NOPE], qg_ref[:, QK_NOPE:QK_PAD]
    kg_n, kg_r = kg_ref[:, 0:QK_NOPE], kg_ref[:, QK_NOPE:QK_PAD]
    scale = QK_HEAD ** -0.5
    for r0 in range(0, x_ref.shape[0], sub):
        rows = slice(r0, r0 + sub)
        yb = _rms(x_ref[rows, :], g_ref[...]).astype(BF16)
        sz_ref[rows, :] = _silu(_dot(yb, wz_ref[...])).astype(BF16)
        cqn = _rms(_dot(yb, wcq_ref[...]), qng_ref[...]).astype(BF16)
        ckvn = _rms(_dot(yb, wckv_ref[...]), kvng_ref[...]).astype(BF16)
        kpe = _dot(yb, wkpe_ref[...])
        v_ref[rows, :] = _dot(ckvn, wuv_ref[...]).astype(BF16)
        q = _dot(cqn, wuq_ref[...])
        kn = _dot(ckvn, wuk_ref[...])
        cs, sn = cs_ref[rows, :], sn_ref[rows, :]

        def rope(u, cs=cs, sn=sn):
            return u * cs + pltpu.roll(u, LANES // 2, axis=1) * sn

        kpe_ss = jnp.sum(kpe * kpe, axis=1, keepdims=True)
        kr = rope(kpe * kg_r)
        for h in range(C_HEADS):
            o = h * QK_PAD
            qn = q[:, o:o + QK_NOPE]
            qr = q[:, o + QK_NOPE:o + QK_PAD]
            r = lax.rsqrt(jnp.sum(qn * qn + qr * qr, axis=1, keepdims=True) * (1.0 / QK_HEAD) + RMS_EPS) * scale
            q_ref[rows, o:o + QK_NOPE] = (qn * r * qg_n).astype(BF16)
            q_ref[rows, o + QK_NOPE:o + QK_PAD] = rope(qr * r * qg_r).astype(BF16)
            knh = kn[:, h * QK_NOPE:(h + 1) * QK_NOPE]
            rk = lax.rsqrt((jnp.sum(knh * knh, axis=1, keepdims=True) + kpe_ss) * (1.0 / QK_HEAD) + RMS_EPS)
            k_ref[rows, o:o + QK_NOPE] = (knh * rk * kg_n).astype(BF16)
            k_ref[rows, o + QK_NOPE:o + QK_PAD] = (kr * rk).astype(BF16)


def _c_in(x2, p, cs, sn, tm, sub):
    n = x2.shape[0]
    tok = lambda w: pl.BlockSpec((tm, w), lambda i: (i, 0))
    names = ("norm_g", "wcq", "wckv", "wkpe", "wz", "qng", "kvng", "wuq", "wuk", "wuv", "qg", "kg")
    return pl.pallas_call(
        functools.partial(_c_in_kernel, sub=sub),
        grid=(n // tm,),
        in_specs=[tok(D_MODEL)] + [_const_spec(p[k].shape) for k in names] + [tok(LANES), tok(LANES)],
        out_specs=[tok(C_HEADS * QK_PAD), tok(C_HEADS * QK_PAD), tok(C_HEADS * V_HEAD), tok(D_MODEL)],
        out_shape=[jax.ShapeDtypeStruct((n, C_HEADS * QK_PAD), BF16),
                   jax.ShapeDtypeStruct((n, C_HEADS * QK_PAD), BF16),
                   jax.ShapeDtypeStruct((n, C_HEADS * V_HEAD), BF16),
                   jax.ShapeDtypeStruct((n, D_MODEL), BF16)],
        compiler_params=_params(("parallel",)),
        name="c_in_proj",
    )(x2, *[p[k] for k in names], cs, sn)


def _attn_kernel(q_ref, k_ref, v_ref, o_ref, *, tq):
    seq = q_ref.shape[0]
    rc = lax.broadcasted_iota(jnp.int32, (tq, tq), 0) // CHUNK
    cc = lax.broadcasted_iota(jnp.int32, (tq, tq), 1) // CHUNK
    mask = cc <= rc
    for qi in range(seq // tq):
        r0 = qi * tq
        q = q_ref[r0:r0 + tq, :]
        s_d = jnp.where(mask, _dot_nt(q, k_ref[r0:r0 + tq, :]), -jnp.inf)
        m = jnp.max(s_d, axis=1, keepdims=True)
        if qi > 0:
            s_o = _dot_nt(q, k_ref[0:r0, :])
            m = jnp.maximum(m, jnp.max(s_o, axis=1, keepdims=True))
        p_d = jnp.exp(s_d - m)
        l = jnp.sum(p_d, axis=1, keepdims=True)
        acc = _dot(p_d.astype(BF16), v_ref[r0:r0 + tq, :])
        if qi > 0:
            p_o = jnp.exp(s_o - m)
            l = l + jnp.sum(p_o, axis=1, keepdims=True)
            acc = acc + _dot(p_o.astype(BF16), v_ref[0:r0, :])
        o_ref[r0:r0 + tq, :] = (acc * (1.0 / l)).astype(BF16)


def _attention(q, k, v, batch, seq, tq):
    n = q.shape[0]
    return pl.pallas_call(
        functools.partial(_attn_kernel, tq=tq),
        grid=(batch, C_HEADS),
        in_specs=[pl.BlockSpec((seq, QK_PAD), lambda b, h: (b, h)),
                  pl.BlockSpec((seq, QK_PAD), lambda b, h: (b, h)),
                  pl.BlockSpec((seq, V_HEAD), lambda b, h: (b, h))],
        out_specs=pl.BlockSpec((seq, V_HEAD), lambda b, h: (b, h)),
        out_shape=jax.ShapeDtypeStruct((n, C_HEADS * V_HEAD), BF16),
        compiler_params=_params(("parallel", "parallel")),
        name="block_causal_attention",
    )(q, k, v)


def _c_out_kernel(x_ref, o_ref, sz_ref, wo_ref, out_ref):
    gated = (o_ref[...].astype(F32) * sz_ref[...].astype(F32)).astype(BF16)
    out_ref[...] = x_ref[...] + _dot(gated, wo_ref[...])


def _c_out(x2, o, sz, w_out, tm):
    n = x2.shape[0]
    tok = lambda w: pl.BlockSpec((tm, w), lambda i: (i, 0))
    return pl.pallas_call(
        _c_out_kernel,
        grid=(n // tm,),
        in_specs=[tok(D_MODEL), tok(D_MODEL), tok(D_MODEL), _const_spec(w_out.shape)],
        out_specs=tok(D_MODEL),
        out_shape=jax.ShapeDtypeStruct((n, D_MODEL), F32),
        compiler_params=_params(("parallel",)),
        name="c_out_proj",
    )(x2, o, sz, w_out)


def _layer_c(x2, p, cs, sn, batch, seq):
    q, k, v, sz = _c_in(x2, p, cs, sn, tm=512, sub=256)
    o = _attention(q, k, v, batch, seq, tq=256)
    return _c_out(x2, o, sz, p["w_out"], tm=512)


def _pad_rope_cols(a):
    half = QK_ROPE // 2
    z = jnp.zeros(a.shape[:-1] + (half,), a.dtype)
    return jnp.concatenate([a[..., :half], z, a[..., half:], z], axis=-1)


def _pad_head_cols(a):
    return jnp.concatenate([a[..., :QK_NOPE], _pad_rope_cols(a[..., QK_NOPE:])], axis=-1)


def _prep_c(norm_g, w_in, q_norm_g, kv_norm_g, w_uq, w_ukv, qn_g, kn_g, w_out):
    c0, c1, c2 = Q_LORA, Q_LORA + KV_LORA, Q_LORA + KV_LORA + QK_ROPE
    wuq = _pad_head_cols(w_uq.reshape(Q_LORA, C_HEADS, QK_HEAD)).reshape(Q_LORA, C_HEADS * QK_PAD)
    wukv = w_ukv.reshape(KV_LORA, C_HEADS, QK_NOPE + V_HEAD)
    return dict(
        norm_g=norm_g[None, :],
        wcq=w_in[:, :c0].astype(BF16), wckv=w_in[:, c0:c1].astype(BF16),
        wkpe=_pad_rope_cols(w_in[:, c1:c2]).astype(BF16), wz=w_in[:, c2:].astype(BF16),
        qng=q_norm_g[None, :], kvng=kv_norm_g[None, :],
        wuq=wuq.astype(BF16),
        wuk=wukv[:, :, :QK_NOPE].reshape(KV_LORA, C_HEADS * QK_NOPE).astype(BF16),
        wuv=wukv[:, :, QK_NOPE:].reshape(KV_LORA, C_HEADS * V_HEAD).astype(BF16),
        qg=_pad_head_cols(qn_g)[None, :], kg=_pad_head_cols(kn_g)[None, :],
        w_out=w_out.astype(BF16))


def kernel(x, positions, a_norm_g, a_w_in, a_b_if, a_conv_w, a_conv_b, a_w_q, a_w_k, a_w_v, a_head_norm_g, a_skip, a_pool_w, a_pool_scale, a_w_out, c_norm_g, c_w_in, c_q_norm_g, c_kv_norm_g, c_w_uq, c_w_ukv, c_qn_g, c_kn_g, c_w_out):
    batch, seq, d = x.shape
    depth = a_norm_g.shape[0] + c_norm_g.shape[0]
    a_args = (a_norm_g, a_w_in, a_b_if, a_conv_w, a_conv_b, a_w_q, a_w_k, a_w_v, a_head_norm_g, a_skip,
              a_pool_w, a_pool_scale, a_w_out)
    c_args = (c_norm_g, c_w_in, c_q_norm_g, c_kv_norm_g, c_w_uq, c_w_ukv, c_qn_g, c_kn_g, c_w_out)
    cos, sin = _rope_tables(positions)
    zeros = jnp.zeros_like(cos)
    cs = jnp.concatenate([cos, zeros, cos, zeros], axis=-1)
    sn = jnp.concatenate([-sin, zeros, sin, zeros], axis=-1)
    x2 = x.reshape(batch * seq, d)
    for layer in range(depth):
        j = layer // 2
        if layer % 2 == 0:
            x2 = _layer_a(x2, _prep_a(*[a[j] for a in a_args]), batch, seq)
        else:
            x2 = _layer_c(x2, _prep_c(*[a[j] for a in c_args]), cs, sn, batch, seq)
    return x2.reshape(batch, seq, d)
```

```python
import functools

import jax
import jax.numpy as jnp
from jax import lax
from jax.experimental import pallas as pl
from jax.experimental.pallas import tpu as pltpu

F32 = jnp.float32
BF16 = jnp.bfloat16

D_MODEL = 1024
RMS_EPS = 1e-6
CHUNK = 64
M_HEADS = 4
M_HEAD_DIM = 256
CONV_K = 4
M_CHUNK = 256
P_WINDOWS = (2, 4, 8, 16)
P_GROUP_DIM = 256
HIST = 16
CONV_HIST = 8
C_HEADS = 8
QK_NOPE = 128
QK_ROPE = 64
QK_HEAD = QK_NOPE + QK_ROPE
QK_PAD = 256
V_HEAD = 128
V_PAD = 256
Q_LORA = 384
KV_LORA = 256
ROPE_THETA = 10000.0
LANES = 128

VMEM_LIMIT = 56 * 1024 * 1024


def _params(sem):
    return pltpu.CompilerParams(dimension_semantics=sem, vmem_limit_bytes=VMEM_LIMIT)


def _const_spec(shape):
    nd = len(shape)
    return pl.BlockSpec(shape, lambda *_: (0,) * nd, pipeline_mode=pl.Buffered(1))


def _rms(x, g):
    return x * lax.rsqrt(jnp.mean(x * x, axis=-1, keepdims=True) + RMS_EPS) * g


def _sigmoid(x):
    return 0.5 * jnp.tanh(0.5 * x) + 0.5


def _silu(x):
    h = 0.5 * x
    return h * jnp.tanh(h) + h


def _log_sigmoid(x):
    return jnp.minimum(x, 0.0) - jnp.log1p(jnp.exp(-jnp.abs(x)))


def _dot(a, b):
    return jnp.dot(a, b, preferred_element_type=F32)


def _dot_nt(a, b):
    return lax.dot_general(a, b, (((1,), (1,)), ((), ())), preferred_element_type=F32)


def _dot_tn(a, b):
    return lax.dot_general(a, b, (((0,), (0,)), ((), ())), preferred_element_type=F32)


def _interleave(chains, skew=0):
    live = dict(enumerate(chains))
    tick = 0
    while live:
        for i in sorted(live):
            if tick >= i * skew:
                try:
                    next(live[i])
                except StopIteration:
                    del live[i]
        tick += 1


def _rope_table_kernel(pos_ref, invf_ref, cos_ref, sin_ref):
    ang = pos_ref[...] * invf_ref[...]
    cos_ref[...] = jnp.cos(ang)
    sin_ref[...] = jnp.sin(ang)


def _rope_tables(positions):
    n = positions.size
    half = QK_ROPE // 2
    per_row = LANES // half
    rows = n // per_row
    inv_freq = ROPE_THETA ** (-jnp.arange(0, QK_ROPE, 2, dtype=F32) / QK_ROPE)
    invf = jnp.tile(inv_freq, per_row)[None, :]
    pos = jnp.repeat(positions.reshape(-1).astype(F32), half).reshape(rows, LANES)
    tr = min(rows, 2048)
    cos, sin = pl.pallas_call(
        _rope_table_kernel,
        grid=(rows // tr,),
        in_specs=[pl.BlockSpec((tr, LANES), lambda i: (i, 0)), _const_spec((1, LANES))],
        out_specs=[pl.BlockSpec((tr, LANES), lambda i: (i, 0))] * 2,
        out_shape=[jax.ShapeDtypeStruct((rows, LANES), F32)] * 2,
        compiler_params=_params(("parallel",)),
        name="rope_tables",
    )(pos, invf)
    return cos.reshape(n, half), sin.reshape(n, half)


def _a_mix_kernel(x_ref, g_ref, w3_ref, wg_ref, wgt_ref, bg_ref, bgt_ref, cw_ref, cb_ref,
                  wq_ref, wk_ref, wv_ref, hng_ref, skip_ref, ya_ref,
                  c_scr, n_scr, m_scr, hist_scr):
    t = x_ref.shape[0]
    mw = M_HEADS * M_HEAD_DIM

    @pl.when(pl.program_id(1) == 0)
    def _():
        c_scr[...] = jnp.zeros_like(c_scr)
        n_scr[...] = jnp.zeros_like(n_scr)
        m_scr[...] = jnp.zeros_like(m_scr)
        hist_scr[...] = jnp.zeros_like(hist_scr)

    yb = _rms(x_ref[...], g_ref[...]).astype(BF16)
    graw = _dot(yb, wg_ref[...]) + bg_ref[...]
    lane = lax.broadcasted_iota(jnp.int32, graw.shape, 1)
    gcol = jnp.where(lane >= M_HEADS, _log_sigmoid(graw), graw)
    grawt = _dot_nt(wgt_ref[...], yb) + bgt_ref[...]
    row = lax.broadcasted_iota(jnp.int32, grawt.shape, 0)
    grow = jnp.where(row >= M_HEADS, _log_sigmoid(grawt), grawt)

    ri = lax.broadcasted_iota(jnp.int32, (t, t), 0)
    ci = lax.broadcasted_iota(jnp.int32, (t, t), 1)
    tril = ci <= ri
    triu = ri <= ci
    neg_inf = -jnp.inf

    def head_chain(h):
        hs = slice(h * M_HEAD_DIM, (h + 1) * M_HEAD_DIM)
        i_row = grow[h:h + 1, :]
        lf_row = grow[M_HEADS + h:M_HEADS + h + 1, :]
        i_col = gcol[:, h:h + 1]
        lf_col = gcol[:, M_HEADS + h:M_HEADS + h + 1]
        b_col = jnp.sum(jnp.where(tril, lf_row, 0.0), axis=1, keepdims=True)
        b_row = jnp.sum(jnp.where(triu, lf_col, 0.0), axis=0, keepdims=True)
        a_row = i_row - b_row
        a_col = i_col - b_col
        m_old = m_scr[h][:, 0:1]
        big_m = jnp.maximum(jnp.max(jnp.where(tril, a_row, neg_inf), axis=1, keepdims=True), m_old)
        decay = jnp.exp(jnp.where(tril, a_row - big_m, neg_inf))
        w_inter = jnp.exp(m_old - big_m)
        m_last = jnp.maximum(jnp.max(a_row, axis=1, keepdims=True), m_old)
        b_last = jnp.sum(lf_row, axis=1, keepdims=True)
        w_state = jnp.exp(a_col - m_last)
        wc = jnp.exp(m_old - m_last)
        inv_floor = jnp.exp(-(b_col + big_m))
        m_scr[h] = jnp.broadcast_to(b_last + m_last, (1, LANES))
        yield
        xm = _dot(yb, w3_ref[:, hs])
        yield
        xe = jnp.concatenate([hist_scr[:, hs], xm], axis=0)
        hist_scr[:, hs] = xm[t - CONV_HIST:, :]
        conv = cb_ref[:, hs] + cw_ref[CONV_K - 1:CONV_K, hs] * xm
        for j in range(CONV_K - 1):
            conv = conv + cw_ref[j:j + 1, hs] * pltpu.roll(xe, CONV_K - 1 - j, axis=0)[CONV_HIST:, :]
        xc = _silu(conv)
        xcb = xc.astype(BF16)
        yield
        q = _dot(xcb, wq_ref[h])
        k = _dot(xcb, wk_ref[h])
        v = _dot(xm.astype(BF16), wv_ref[h])
        qb, vb = q.astype(BF16), v.astype(BF16)
        yield
        s = _dot_nt(qb, k.astype(BF16)) * decay
        n_row = n_scr[h]
        den = (jnp.sum(s, axis=1, keepdims=True)
               + w_inter * jnp.sum(q * n_row, axis=1, keepdims=True))
        yield
        num = _dot(s.astype(BF16), vb) + w_inter * _dot(qb, c_scr[h].astype(BF16))
        hh = num * (1.0 / jnp.maximum(jnp.abs(den), inv_floor))
        kw = k * w_state
        c_scr[h] = wc * c_scr[h] + _dot_tn(kw.astype(BF16), vb)
        n_scr[h] = wc * n_row + jnp.sum(kw, axis=0, keepdims=True)
        yield
        om = _dot(yb, w3_ref[:, mw + h * M_HEAD_DIM:mw + (h + 1) * M_HEAD_DIM])
        zm = _dot(yb, w3_ref[:, 2 * mw + h * M_HEAD_DIM:2 * mw + (h + 1) * M_HEAD_DIM])
        yield
        hm = _rms(hh * _sigmoid(om), hng_ref[:, hs])
        ya_ref[:, hs] = ((hm + skip_ref[:, hs] * xc) * _silu(zm)).astype(BF16)

    _interleave([head_chain(h) for h in range(M_HEADS)])


def _a_mix(x2, p, batch, seq):
    n = x2.shape[0]
    t = M_CHUNK
    spt = seq // t
    tok = lambda w: pl.BlockSpec((t, w), lambda b, s: (b * spt + s, 0))
    names = ("norm_g", "w3", "wg", "wgt", "bg", "bgt", "conv_w", "conv_b", "wq", "wk", "wv", "hng", "skip")
    return pl.pallas_call(
        _a_mix_kernel,
        grid=(batch, spt),
        in_specs=[tok(D_MODEL)] + [_const_spec(p[k].shape) for k in names],
        out_specs=tok(D_MODEL),
        out_shape=jax.ShapeDtypeStruct((n, D_MODEL), BF16),
        scratch_shapes=[pltpu.VMEM((M_HEADS, M_HEAD_DIM, M_HEAD_DIM), F32),
                        pltpu.VMEM((M_HEADS, 1, M_HEAD_DIM), F32),
                        pltpu.VMEM((M_HEADS, 1, LANES), F32),
                        pltpu.VMEM((CONV_HIST, D_MODEL), F32)],
        compiler_params=_params(("parallel", "arbitrary")),
        name="a_mlstm",
    )(x2, *[p[k] for k in names])


def _a_out_kernel(x_ref, ya_ref, g_ref, wp_ref, pw_ref, ps_ref, wo_ref, o_ref, hist_scr):
    tm = x_ref.shape[0]
    si = pl.program_id(1)

    @pl.when(si == 0)
    def _():
        hist_scr[...] = jnp.zeros_like(hist_scr)

    x = x_ref[...]
    yb = _rms(x, g_ref[...]).astype(BF16)
    tpos = si * tm + lax.broadcasted_iota(jnp.int32, (tm, 1), 0)
    pw_all = len(P_WINDOWS) * P_GROUP_DIM
    parts = [None] * (len(P_WINDOWS) + 1)

    def skip_chain():
        parts[0] = x + _dot(ya_ref[...], wo_ref[0:D_MODEL, :])
        yield

    def group_chain(g, w):
        gs = slice(g * P_GROUP_DIM, (g + 1) * P_GROUP_DIM)
        xp = _dot(yb, wp_ref[:, gs])
        zp = _dot(yb, wp_ref[:, pw_all + g * P_GROUP_DIM:pw_all + (g + 1) * P_GROUP_DIM])
        yield
        xe = jnp.concatenate([hist_scr[:, gs], xp], axis=0)
        hist_scr[:, gs] = xp[tm - HIST:, :]
        tot = xe
        sh = 1
        while sh < w:
            tot = tot + pltpu.roll(tot, sh, axis=0)
            sh *= 2
        inv_cnt = 1.0 / jnp.minimum(tpos + 1, w).astype(F32)
        mix = (tot[HIST:, :] * inv_cnt - xp).astype(BF16)
        gate = ps_ref[:, gs] * _silu(zp)
        yield
        yb_g = (_dot(mix, pw_ref[g]) * gate).astype(BF16)
        yield
        parts[g + 1] = _dot(yb_g, wo_ref[D_MODEL + g * P_GROUP_DIM:D_MODEL + (g + 1) * P_GROUP_DIM, :])
        yield

    _interleave([skip_chain()] + [group_chain(g, w) for g, w in enumerate(P_WINDOWS)])
    acc = parts[0]
    for part in parts[1:]:
        acc = acc + part
    o_ref[...] = acc


def _a_out(x2, ya, p, batch, seq, tm):
    n = x2.shape[0]
    spt = seq // tm
    tok = lambda w: pl.BlockSpec((tm, w), lambda b, s: (b * spt + s, 0))
    names = ("norm_g", "wp", "pool_w", "pool_scale", "w_out")
    return pl.pallas_call(
        _a_out_kernel,
        grid=(batch, spt),
        in_specs=[tok(D_MODEL), tok(D_MODEL)] + [_const_spec(p[k].shape) for k in names],
        out_specs=tok(D_MODEL),
        out_shape=jax.ShapeDtypeStruct((n, D_MODEL), F32),
        scratch_shapes=[pltpu.VMEM((HIST, D_MODEL), F32)],
        compiler_params=_params(("parallel", "arbitrary")),
        name="a_pool_out_proj",
    )(x2, ya, *[p[k] for k in names])


def _layer_a(x2, p, batch, seq):
    ya = _a_mix(x2, p, batch, seq)
    return _a_out(x2, ya, p, batch, seq, tm=256)


def _prep_a(norm_g, w_in, b_if, conv_w, conv_b, w_q, w_k, w_v, head_norm_g, skip, pool_w, pool_scale, w_out):
    mw = M_HEADS * M_HEAD_DIM
    g0 = 2 * mw
    g1 = g0 + 2 * M_HEADS
    z1 = g1 + mw
    wg = w_in[:, g0:g1]
    pad = LANES - 2 * M_HEADS
    return dict(
        norm_g=norm_g[None, :],
        w3=jnp.concatenate([w_in[:, :g0], w_in[:, g1:z1]], axis=1).astype(BF16),
        wp=w_in[:, z1:].astype(BF16),
        wg=jnp.pad(wg, ((0, 0), (0, pad))).astype(BF16), wgt=wg.T.astype(BF16),
        bg=jnp.pad(b_if, (0, pad))[None, :], bgt=b_if[:, None],
        conv_w=conv_w, conv_b=conv_b[None, :],
        wq=w_q.astype(BF16), wk=(w_k * (M_HEAD_DIM ** -0.5)).astype(BF16), wv=w_v.astype(BF16),
        hng=head_norm_g[None, :], skip=skip[None, :],
        pool_w=pool_w.astype(BF16), pool_scale=pool_scale[None, :], w_out=w_out.astype(BF16))


def _c_in_kernel(x_ref, g_ref, wcq_ref, wckv_ref, wkpe_ref, wz_ref, qng_ref, kvng_ref,
                 wuq_ref, wuk_ref, wuv_ref, qg_ref, kg_ref, cs_ref, sn_ref,
                 q_ref, k_ref, v_ref, sz_ref, *, sub):
    qg_n, qg_r = qg_ref[:, 0:QK_NOPE], qg_ref[:, QK_NOPE:QK_PAD]
    kg_n, kg_r = kg_ref[:, 0:QK_NOPE], kg_ref[:, QK_NOPE:QK_PAD]
    scale = QK_HEAD ** -0.5

    def rows_chain(r0):
        rows = slice(r0, r0 + sub)
        yb = _rms(x_ref[rows, :], g_ref[...]).astype(BF16)
        yield
        z = _dot(yb, wz_ref[...])
        cq = _dot(yb, wcq_ref[...])
        ckv = _dot(yb, wckv_ref[...])
        kpe = _dot(yb, wkpe_ref[...])
        yield
        cqn = _rms(cq, qng_ref[...]).astype(BF16)
        ckvn = _rms(ckv, kvng_ref[...]).astype(BF16)
        sz_ref[rows, :] = _silu(z).astype(BF16)
        cs, sn = cs_ref[rows, :], sn_ref[rows, :]

        def rope(u):
            return u * cs + pltpu.roll(u, LANES // 2, axis=1) * sn

        kpe_ss = jnp.sum(kpe * kpe, axis=1, keepdims=True)
        kr = rope(kpe * kg_r)
        yield
        q = _dot(cqn, wuq_ref[...])
        kn = _dot(ckvn, wuk_ref[...])
        v = _dot(ckvn, wuv_ref[...]).astype(BF16)
        yield
        ones_col = (lax.broadcasted_iota(jnp.int32, (sub, V_PAD - V_HEAD), 1) == 0).astype(BF16)
        for h in range(C_HEADS):
            v_ref[rows, h * V_PAD:h * V_PAD + V_HEAD] = v[:, h * V_HEAD:(h + 1) * V_HEAD]
            v_ref[rows, h * V_PAD + V_HEAD:(h + 1) * V_PAD] = ones_col
            o = h * QK_PAD
            qn = q[:, o:o + QK_NOPE]
            qr = q[:, o + QK_NOPE:o + QK_PAD]
            r = lax.rsqrt(jnp.sum(qn * qn + qr * qr, axis=1, keepdims=True) * (1.0 / QK_HEAD) + RMS_EPS) * scale
            q_ref[rows, o:o + QK_NOPE] = (qn * r * qg_n).astype(BF16)
            q_ref[rows, o + QK_NOPE:o + QK_PAD] = rope(qr * r * qg_r).astype(BF16)
            knh = kn[:, h * QK_NOPE:(h + 1) * QK_NOPE]
            rk = lax.rsqrt((jnp.sum(knh * knh, axis=1, keepdims=True) + kpe_ss) * (1.0 / QK_HEAD) + RMS_EPS)
            k_ref[rows, o:o + QK_NOPE] = (knh * rk * kg_n).astype(BF16)
            k_ref[rows, o + QK_NOPE:o + QK_PAD] = (kr * rk).astype(BF16)
            if h % 2 == 1:
                yield

    _interleave([rows_chain(r0) for r0 in range(0, x_ref.shape[0], sub)], skew=2)


def _c_in(x2, p, cs, sn, tm, sub):
    n = x2.shape[0]
    tok = lambda w: pl.BlockSpec((tm, w), lambda i: (i, 0))
    names = ("norm_g", "wcq", "wckv", "wkpe", "wz", "qng", "kvng", "wuq", "wuk", "wuv", "qg", "kg")
    return pl.pallas_call(
        functools.partial(_c_in_kernel, sub=sub),
        grid=(n // tm,),
        in_specs=[tok(D_MODEL)] + [_const_spec(p[k].shape) for k in names] + [tok(LANES), tok(LANES)],
        out_specs=[tok(C_HEADS * QK_PAD), tok(C_HEADS * QK_PAD), tok(C_HEADS * V_PAD), tok(D_MODEL)],
        out_shape=[jax.ShapeDtypeStruct((n, C_HEADS * QK_PAD), BF16),
                   jax.ShapeDtypeStruct((n, C_HEADS * QK_PAD), BF16),
                   jax.ShapeDtypeStruct((n, C_HEADS * V_PAD), BF16),
                   jax.ShapeDtypeStruct((n, D_MODEL), BF16)],
        compiler_params=_params(("parallel",)),
        name="c_in_proj",
    )(x2, *[p[k] for k in names], cs, sn)


def _attn_kernel(q_ref, k_ref, v_ref, o_ref, *, tq, heads):
    seq = q_ref.shape[0]
    nq = seq // tq
    rc = lax.broadcasted_iota(jnp.int32, (tq, tq), 0) // CHUNK
    cc = lax.broadcasted_iota(jnp.int32, (tq, tq), 1) // CHUNK
    mask = cc <= rc

    def row_max(blocks):
        m = blocks[0]
        for blk in blocks[1:]:
            m = jnp.maximum(m, blk)
        return jnp.max(m, axis=1, keepdims=True)

    def head_chain(h):
        qc = slice(h * QK_PAD, (h + 1) * QK_PAD)
        vc = slice(h * V_PAD, (h + 1) * V_PAD)

        def scores(qi, c):
            s = _dot_nt(q_ref[qi * tq:(qi + 1) * tq, qc], k_ref[c * tq:(c + 1) * tq, qc])
            return jnp.where(mask, s, -jnp.inf) if c == qi else s

        order = list(range(nq - 1, -1, -1))
        cur = [scores(order[0], c) for c in range(order[0] + 1)]
        yield
        for pos, qi in enumerate(order):
            m = row_max(cur)
            nq_i = order[pos + 1] if pos + 1 < nq else None
            nxt = []
            pb = []
            for c in range(qi + 1):
                pb.append(jnp.exp(cur[c] - m).astype(BF16))
                if nq_i is not None and c <= nq_i:
                    nxt.append(scores(nq_i, c))
            p_all = pb[0] if len(pb) == 1 else jnp.concatenate(pb, axis=1)
            acc = _dot(p_all, v_ref[0:(qi + 1) * tq, vc])
            o_ref[qi * tq:(qi + 1) * tq, h * V_HEAD:(h + 1) * V_HEAD] = (
                acc[:, 0:V_HEAD] * (1.0 / acc[:, V_HEAD:V_HEAD + 1])).astype(BF16)
            cur = nxt
            yield

    _interleave([head_chain(h) for h in range(heads)])


def _attention(q, k, v, batch, seq, tq, heads):
    n = q.shape[0]
    return pl.pallas_call(
        functools.partial(_attn_kernel, tq=tq, heads=heads),
        grid=(batch, C_HEADS // heads),
        in_specs=[pl.BlockSpec((seq, heads * QK_PAD), lambda b, h: (b, h)),
                  pl.BlockSpec((seq, heads * QK_PAD), lambda b, h: (b, h)),
                  pl.BlockSpec((seq, heads * V_PAD), lambda b, h: (b, h))],
        out_specs=pl.BlockSpec((seq, heads * V_HEAD), lambda b, h: (b, h)),
        out_shape=jax.ShapeDtypeStruct((n, C_HEADS * V_HEAD), BF16),
        compiler_params=_params(("parallel", "parallel")),
        name="block_causal_attention",
    )(q, k, v)


def _c_out_kernel(x_ref, o_ref, sz_ref, wo_ref, out_ref):
    gated = (o_ref[...].astype(F32) * sz_ref[...].astype(F32)).astype(BF16)
    out_ref[...] = x_ref[...] + _dot(gated, wo_ref[...])


def _c_out(x2, o, sz, w_out, tm):
    n = x2.shape[0]
    tok = lambda w: pl.BlockSpec((tm, w), lambda i: (i, 0))
    return pl.pallas_call(
        _c_out_kernel,
        grid=(n // tm,),
        in_specs=[tok(D_MODEL), tok(D_MODEL), tok(D_MODEL), _const_spec(w_out.shape)],
        out_specs=tok(D_MODEL),
        out_shape=jax.ShapeDtypeStruct((n, D_MODEL), F32),
        compiler_params=_params(("parallel",)),
        name="c_out_proj",
    )(x2, o, sz, w_out)


def _layer_c(x2, p, cs, sn, batch, seq):
    q, k, v, sz = _c_in(x2, p, cs, sn, tm=512, sub=128)
    o = _attention(q, k, v, batch, seq, tq=256, heads=2)
    return _c_out(x2, o, sz, p["w_out"], tm=512)


def _pad_rope_cols(a):
    half = QK_ROPE // 2
    z = jnp.zeros(a.shape[:-1] + (half,), a.dtype)
    return jnp.concatenate([a[..., :half], z, a[..., half:], z], axis=-1)


def _pad_head_cols(a):
    return jnp.concatenate([a[..., :QK_NOPE], _pad_rope_cols(a[..., QK_NOPE:])], axis=-1)


def _prep_c(norm_g, w_in, q_norm_g, kv_norm_g, w_uq, w_ukv, qn_g, kn_g, w_out):
    c0, c1, c2 = Q_LORA, Q_LORA + KV_LORA, Q_LORA + KV_LORA + QK_ROPE
    wuq = _pad_head_cols(w_uq.reshape(Q_LORA, C_HEADS, QK_HEAD)).reshape(Q_LORA, C_HEADS * QK_PAD)
    wukv = w_ukv.reshape(KV_LORA, C_HEADS, QK_NOPE + V_HEAD)
    return dict(
        norm_g=norm_g[None, :],
        wcq=w_in[:, :c0].astype(BF16), wckv=w_in[:, c0:c1].astype(BF16),
        wkpe=_pad_rope_cols(w_in[:, c1:c2]).astype(BF16), wz=w_in[:, c2:].astype(BF16),
        qng=q_norm_g[None, :], kvng=kv_norm_g[None, :],
        wuq=wuq.astype(BF16),
        wuk=wukv[:, :, :QK_NOPE].reshape(KV_LORA, C_HEADS * QK_NOPE).astype(BF16),
        wuv=wukv[:, :, QK_NOPE:].reshape(KV_LORA, C_HEADS * V_HEAD).astype(BF16),
        qg=_pad_head_cols(qn_g)[None, :], kg=_pad_head_cols(kn_g)[None, :],
        w_out=w_out.astype(BF16))


def kernel(x, positions, a_norm_g, a_w_in, a_b_if, a_conv_w, a_conv_b, a_w_q, a_w_k, a_w_v, a_head_norm_g, a_skip, a_pool_w, a_pool_scale, a_w_out, c_norm_g, c_w_in, c_q_norm_g, c_kv_norm_g, c_w_uq, c_w_ukv, c_qn_g, c_kn_g, c_w_out):
    batch, seq, d = x.shape
    depth = a_norm_g.shape[0] + c_norm_g.shape[0]
    a_args = (a_norm_g, a_w_in, a_b_if, a_conv_w, a_conv_b, a_w_q, a_w_k, a_w_v, a_head_norm_g, a_skip,
              a_pool_w, a_pool_scale, a_w_out)
    c_args = (c_norm_g, c_w_in, c_q_norm_g, c_kv_norm_g, c_w_uq, c_w_ukv, c_qn_g, c_kn_g, c_w_out)
    cos, sin = _rope_tables(positions)
    zeros = jnp.zeros_like(cos)
    cs = jnp.concatenate([cos, zeros, cos, zeros], axis=-1)
    sn = jnp.concatenate([-sin, zeros, sin, zeros], axis=-1)
    x2 = x.reshape(batch * seq, d)
    for layer in range(depth):
        j = layer // 2
        if layer % 2 == 0:
            x2 = _layer_a(x2, _prep_a(*[a[j] for a in a_args]), batch, seq)
        else:
            x2 = _layer_c(x2, _prep_c(*[a[j] for a in c_args]), cs, sn, batch, seq)
    return x2.reshape(batch, seq, d)
```

```python
import functools

import jax
import jax.numpy as jnp
from jax import lax
from jax.experimental import pallas as pl
from jax.experimental.pallas import tpu as pltpu

F32 = jnp.float32
BF16 = jnp.bfloat16

D_MODEL = 1024
RMS_EPS = 1e-6
CHUNK = 64
M_HEADS = 4
M_HEAD_DIM = 256
CONV_K = 4
M_CHUNK = 256
P_WINDOWS = (2, 4, 8, 16)
P_GROUP_DIM = 256
HIST = 16
CONV_HIST = 8
C_HEADS = 8
QK_NOPE = 128
QK_ROPE = 64
QK_HEAD = QK_NOPE + QK_ROPE
QK_PAD = 256
V_HEAD = 128
V_PAD = 256
Q_LORA = 384
KV_LORA = 256
ROPE_THETA = 10000.0
LANES = 128

VMEM_LIMIT = 56 * 1024 * 1024


def _params(sem):
    return pltpu.CompilerParams(dimension_semantics=sem, vmem_limit_bytes=VMEM_LIMIT)


def _const_spec(shape):
    nd = len(shape)
    return pl.BlockSpec(shape, lambda *_: (0,) * nd, pipeline_mode=pl.Buffered(1))


def _rms(x, g):
    return x * lax.rsqrt(jnp.mean(x * x, axis=-1, keepdims=True) + RMS_EPS) * g


def _sigmoid(x):
    return 0.5 * jnp.tanh(0.5 * x) + 0.5


def _silu(x):
    h = 0.5 * x
    return h * jnp.tanh(h) + h


def _log_sigmoid(x):
    return jnp.minimum(x, 0.0) - jnp.log1p(jnp.exp(-jnp.abs(x)))


def _dot(a, b):
    return jnp.dot(a, b, preferred_element_type=F32)


def _dot_nt(a, b):
    return lax.dot_general(a, b, (((1,), (1,)), ((), ())), preferred_element_type=F32)


def _dot_tn(a, b):
    return lax.dot_general(a, b, (((0,), (0,)), ((), ())), preferred_element_type=F32)


def _interleave(chains, skew=0):
    live = dict(enumerate(chains))
    tick = 0
    while live:
        for i in sorted(live):
            if tick >= i * skew:
                try:
                    next(live[i])
                except StopIteration:
                    del live[i]
        tick += 1


def _rope_table_kernel(pos_ref, invf_ref, tab_ref):
    half = QK_ROPE // 2
    groups = LANES // half
    ang = pos_ref[...] * invf_ref[...]
    cos, sin = jnp.cos(ang), jnp.sin(ang)
    group = lax.broadcasted_iota(jnp.int32, ang.shape, 1) // half
    rolled = {0: (cos, sin)}
    for sh in range(half, LANES, half):
        rolled[sh] = (pltpu.roll(cos, sh, axis=1), pltpu.roll(sin, sh, axis=1))
    for k in range(groups):
        at = [rolled[((g - k) * half) % LANES] for g in range(groups)]
        tab_ref[k] = jnp.where(group == 0, at[0][0],
                               jnp.where(group == 1, at[1][0],
                                         jnp.where(group == 2, -at[2][1], at[3][1])))


def _rope_table(positions):
    n = positions.size
    half = QK_ROPE // 2
    per_row = LANES // half
    rows = n // per_row
    inv_freq = ROPE_THETA ** (-jnp.arange(0, QK_ROPE, 2, dtype=F32) / QK_ROPE)
    invf = jnp.tile(inv_freq, per_row)[None, :]
    pos = positions.reshape(per_row, rows).astype(F32).T
    pos = jnp.broadcast_to(pos[:, :, None], (rows, per_row, half)).reshape(rows, LANES)
    tr = min(rows, 2048)
    tab = pl.pallas_call(
        _rope_table_kernel,
        grid=(rows // tr,),
        in_specs=[pl.BlockSpec((tr, LANES), lambda i: (i, 0)), _const_spec((1, LANES))],
        out_specs=pl.BlockSpec((per_row, tr, LANES), lambda i: (0, i, 0)),
        out_shape=jax.ShapeDtypeStruct((per_row, rows, LANES), F32),
        compiler_params=_params(("parallel",)),
        name="rope_table",
    )(pos, invf)
    return tab.reshape(n, LANES)


def _a_mix_kernel(x_ref, g_ref, w3_ref, wg_ref, wgt_ref, bg_ref, bgt_ref, cw_ref, cb_ref,
                  wq_ref, wk_ref, wv_ref, hng_ref, skip_ref, ya_ref,
                  c_scr, n_scr, m_scr, hist_scr):
    t = M_CHUNK
    mw = M_HEADS * M_HEAD_DIM

    @pl.when(pl.program_id(1) == 0)
    def _():
        c_scr[...] = jnp.zeros_like(c_scr)
        n_scr[...] = jnp.zeros_like(n_scr)
        m_scr[...] = jnp.zeros_like(m_scr)
        hist_scr[...] = jnp.zeros_like(hist_scr)

    ri = lax.broadcasted_iota(jnp.int32, (t, t), 0)
    ci = lax.broadcasted_iota(jnp.int32, (t, t), 1)
    tril = ci <= ri
    triu = ri <= ci
    neg_inf = -jnp.inf

    def chunk_inputs(rows):
        yb = _rms(x_ref[rows, :], g_ref[...]).astype(BF16)
        graw = _dot(yb, wg_ref[...]) + bg_ref[...]
        lane = lax.broadcasted_iota(jnp.int32, graw.shape, 1)
        gcol = jnp.where(lane >= M_HEADS, _log_sigmoid(graw), graw)
        grawt = _dot_nt(wgt_ref[...], yb) + bgt_ref[...]
        row = lax.broadcasted_iota(jnp.int32, grawt.shape, 0)
        grow = jnp.where(row >= M_HEADS, _log_sigmoid(grawt), grawt)
        return yb, gcol, grow

    row_slices = [slice(r0, r0 + t) for r0 in range(0, x_ref.shape[0], t)]
    inputs = [chunk_inputs(rows) for rows in row_slices]

    def head_chain(h):
        hs = slice(h * M_HEAD_DIM, (h + 1) * M_HEAD_DIM)
        for rows, (yb, gcol, grow) in zip(row_slices, inputs):
            i_row = grow[h:h + 1, :]
            lf_row = grow[M_HEADS + h:M_HEADS + h + 1, :]
            i_col = gcol[:, h:h + 1]
            lf_col = gcol[:, M_HEADS + h:M_HEADS + h + 1]
            b_col = jnp.sum(jnp.where(tril, lf_row, 0.0), axis=1, keepdims=True)
            b_row = jnp.sum(jnp.where(triu, lf_col, 0.0), axis=0, keepdims=True)
            a_row = i_row - b_row
            a_col = i_col - b_col
            m_old = m_scr[h][:, 0:1]
            big_m = jnp.maximum(jnp.max(jnp.where(tril, a_row, neg_inf), axis=1, keepdims=True), m_old)
            decay = jnp.exp(jnp.where(tril, a_row - big_m, neg_inf))
            w_inter = jnp.exp(m_old - big_m)
            m_last = jnp.maximum(jnp.max(a_row, axis=1, keepdims=True), m_old)
            b_last = jnp.sum(lf_row, axis=1, keepdims=True)
            w_state = jnp.exp(a_col - m_last)
            wc = jnp.exp(m_old - m_last)
            inv_floor = jnp.exp(-(b_col + big_m))
            m_scr[h] = jnp.broadcast_to(b_last + m_last, (1, LANES))
            yield
            xm = _dot(yb, w3_ref[:, hs])
            yield
            xe = jnp.concatenate([hist_scr[:, hs], xm], axis=0)
            hist_scr[:, hs] = xm[t - CONV_HIST:, :]
            x1 = pltpu.roll(xe, 1, axis=0)
            u = cw_ref[3:4, hs] * xe + cw_ref[2:3, hs] * x1
            v2 = pltpu.roll(cw_ref[1:2, hs] * xe + cw_ref[0:1, hs] * x1, 2, axis=0)
            conv = cb_ref[:, hs] + (u + v2)[CONV_HIST:, :]
            xc = _silu(conv)
            xcb = xc.astype(BF16)
            yield
            q = _dot(xcb, wq_ref[h])
            k = _dot(xcb, wk_ref[h])
            v = _dot(xm.astype(BF16), wv_ref[h])
            qb, vb = q.astype(BF16), v.astype(BF16)
            yield
            s = _dot_nt(qb, k.astype(BF16)) * decay
            n_row = n_scr[h]
            den = (jnp.sum(s, axis=1, keepdims=True)
                   + w_inter * jnp.sum(q * n_row, axis=1, keepdims=True))
            yield
            num = _dot(s.astype(BF16), vb) + w_inter * _dot(qb, c_scr[h].astype(BF16))
            hh = num * (1.0 / jnp.maximum(jnp.abs(den), inv_floor))
            kw = k * w_state
            c_scr[h] = wc * c_scr[h] + _dot_tn(kw.astype(BF16), vb)
            n_scr[h] = wc * n_row + jnp.sum(kw, axis=0, keepdims=True)
            yield
            om = _dot(yb, w3_ref[:, mw + h * M_HEAD_DIM:mw + (h + 1) * M_HEAD_DIM])
            zm = _dot(yb, w3_ref[:, 2 * mw + h * M_HEAD_DIM:2 * mw + (h + 1) * M_HEAD_DIM])
            yield
            hm = _rms(hh * _sigmoid(om), hng_ref[:, hs])
            ya_ref[rows, hs] = ((hm + skip_ref[:, hs] * xc) * _silu(zm)).astype(BF16)

    _interleave([head_chain(h) for h in range(M_HEADS)])


def _a_mix(x2, p, batch, seq):
    n = x2.shape[0]
    t = 2 * M_CHUNK
    spt = seq // t
    tok = lambda w: pl.BlockSpec((t, w), lambda b, s: (b * spt + s, 0))
    names = ("norm_g", "w3", "wg", "wgt", "bg", "bgt", "conv_w", "conv_b", "wq", "wk", "wv", "hng", "skip")
    return pl.pallas_call(
        _a_mix_kernel,
        grid=(batch, spt),
        in_specs=[tok(D_MODEL)] + [_const_spec(p[k].shape) for k in names],
        out_specs=tok(D_MODEL),
        out_shape=jax.ShapeDtypeStruct((n, D_MODEL), BF16),
        scratch_shapes=[pltpu.VMEM((M_HEADS, M_HEAD_DIM, M_HEAD_DIM), F32),
                        pltpu.VMEM((M_HEADS, 1, M_HEAD_DIM), F32),
                        pltpu.VMEM((M_HEADS, 1, LANES), F32),
                        pltpu.VMEM((CONV_HIST, D_MODEL), F32)],
        compiler_params=_params(("parallel", "arbitrary")),
        name="a_mlstm",
    )(x2, *[p[k] for k in names])


def _a_out_kernel(x_ref, ya_ref, g_ref, wp_ref, pw_ref, ps_ref, wo_ref, o_ref, hist_scr, *, sub):
    tm = x_ref.shape[0]
    si = pl.program_id(1)

    @pl.when(si == 0)
    def _():
        hist_scr[...] = jnp.zeros_like(hist_scr)

    pw_all = len(P_WINDOWS) * P_GROUP_DIM
    row_slices = [slice(r0, r0 + sub) for r0 in range(0, tm, sub)]
    parts = [[None] * (len(P_WINDOWS) + 1) for _ in row_slices]
    xs = [x_ref[rows, :] for rows in row_slices]
    ybs = [_rms(x, g_ref[...]).astype(BF16) for x in xs]

    def skip_chain():
        for i, rows in enumerate(row_slices):
            parts[i][0] = xs[i] + _dot(ya_ref[rows, :], wo_ref[0:D_MODEL, :])
            yield

    def group_chain(g, w):
        gs = slice(g * P_GROUP_DIM, (g + 1) * P_GROUP_DIM)
        for i, rows in enumerate(row_slices):
            yb = ybs[i]
            tpos = si * tm + rows.start + lax.broadcasted_iota(jnp.int32, (sub, 1), 0)
            xp = _dot(yb, wp_ref[:, gs])
            zp = _dot(yb, wp_ref[:, pw_all + g * P_GROUP_DIM:pw_all + (g + 1) * P_GROUP_DIM])
            yield
            xe = jnp.concatenate([hist_scr[:, gs], xp], axis=0)
            hist_scr[:, gs] = xp[sub - HIST:, :]
            tot = xe
            sh = 1
            while sh < w:
                tot = tot + pltpu.roll(tot, sh, axis=0)
                sh *= 2
            inv_cnt = 1.0 / jnp.minimum(tpos + 1, w).astype(F32)
            mix = (tot[HIST:, :] * inv_cnt - xp).astype(BF16)
            gate = ps_ref[:, gs] * _silu(zp)
            yield
            yb_g = (_dot(mix, pw_ref[g]) * gate).astype(BF16)
            yield
            parts[i][g + 1] = _dot(yb_g, wo_ref[D_MODEL + g * P_GROUP_DIM:D_MODEL + (g + 1) * P_GROUP_DIM, :])
            yield

    _interleave([skip_chain()] + [group_chain(g, w) for g, w in enumerate(P_WINDOWS)])
    for i, rows in enumerate(row_slices):
        acc = parts[i][0]
        for part in parts[i][1:]:
            acc = acc + part
        o_ref[rows, :] = acc


def _a_out(x2, ya, p, batch, seq, tm, sub):
    n = x2.shape[0]
    spt = seq // tm
    tok = lambda w: pl.BlockSpec((tm, w), lambda b, s: (b * spt + s, 0))
    names = ("norm_g", "wp", "pool_w", "pool_scale", "w_out")
    return pl.pallas_call(
        functools.partial(_a_out_kernel, sub=sub),
        grid=(batch, spt),
        in_specs=[tok(D_MODEL), tok(D_MODEL)] + [_const_spec(p[k].shape) for k in names],
        out_specs=tok(D_MODEL),
        out_shape=jax.ShapeDtypeStruct((n, D_MODEL), F32),
        scratch_shapes=[pltpu.VMEM((HIST, D_MODEL), F32)],
        compiler_params=_params(("parallel", "arbitrary")),
        name="a_pool_out_proj",
    )(x2, ya, *[p[k] for k in names])


def _layer_a(x2, p, batch, seq):
    ya = _a_mix(x2, p, batch, seq)
    return _a_out(x2, ya, p, batch, seq, tm=512, sub=256)


def _prep_a(norm_g, w_in, b_if, conv_w, conv_b, w_q, w_k, w_v, head_norm_g, skip, pool_w, pool_scale, w_out):
    mw = M_HEADS * M_HEAD_DIM
    g0 = 2 * mw
    g1 = g0 + 2 * M_HEADS
    z1 = g1 + mw
    wg = w_in[:, g0:g1]
    pad = LANES - 2 * M_HEADS
    return dict(
        norm_g=norm_g[None, :],
        w3=jnp.concatenate([w_in[:, :g0], w_in[:, g1:z1]], axis=1).astype(BF16),
        wp=w_in[:, z1:].astype(BF16),
        wg=jnp.pad(wg, ((0, 0), (0, pad))).astype(BF16), wgt=wg.T.astype(BF16),
        bg=jnp.pad(b_if, (0, pad))[None, :], bgt=b_if[:, None],
        conv_w=conv_w, conv_b=conv_b[None, :],
        wq=w_q.astype(BF16), wk=(w_k * (M_HEAD_DIM ** -0.5)).astype(BF16), wv=w_v.astype(BF16),
        hng=head_norm_g[None, :], skip=skip[None, :],
        pool_w=pool_w.astype(BF16), pool_scale=pool_scale[None, :], w_out=w_out.astype(BF16))


def _c_in_kernel(x_ref, g_ref, wcq_ref, wckv_ref, wkpe_ref, wz_ref, qng_ref, kvng_ref,
                 wuq_ref, wuk_ref, wuv_ref, qg_ref, kg_ref, tab_ref,
                 q_ref, k_ref, v_ref, sz_ref, *, sub):
    qg_n, qg_r = qg_ref[:, 0:QK_NOPE], qg_ref[:, QK_NOPE:QK_PAD]
    kg_n, kg_r = kg_ref[:, 0:QK_NOPE], kg_ref[:, QK_NOPE:QK_PAD]
    scale = QK_HEAD ** -0.5

    def rows_chain(r0):
        rows = slice(r0, r0 + sub)
        yb = _rms(x_ref[rows, :], g_ref[...]).astype(BF16)
        yield
        z = _dot(yb, wz_ref[...])
        cq = _dot(yb, wcq_ref[...])
        ckv = _dot(yb, wckv_ref[...])
        kpe = _dot(yb, wkpe_ref[...])
        yield
        cqn = _rms(cq, qng_ref[...]).astype(BF16)
        ckvn = _rms(ckv, kvng_ref[...]).astype(BF16)
        sz_ref[rows, :] = _silu(z).astype(BF16)
        tab = tab_ref[rows, :]
        q_tab = tab * qg_r
        kpe_ss = 0.5 * jnp.sum(kpe * kpe, axis=1, keepdims=True)
        kp = kpe * (tab * kg_r)
        kr = kp + pltpu.roll(kp, LANES // 2, axis=1)
        yield
        q = _dot(cqn, wuq_ref[...])
        kn = _dot(ckvn, wuk_ref[...])
        v = _dot(ckvn, wuv_ref[...]).astype(BF16)
        yield
        ones_col = (lax.broadcasted_iota(jnp.int32, (sub, V_PAD - V_HEAD), 1) == 0).astype(BF16)
        for h in range(C_HEADS):
            v_ref[rows, h * V_PAD:h * V_PAD + V_HEAD] = v[:, h * V_HEAD:(h + 1) * V_HEAD]
            v_ref[rows, h * V_PAD + V_HEAD:(h + 1) * V_PAD] = ones_col
            o = h * QK_PAD
            qn = q[:, o:o + QK_NOPE]
            qr = q[:, o + QK_NOPE:o + QK_PAD]
            ss = jnp.sum(qn * qn + 0.5 * (qr * qr), axis=1, keepdims=True)
            r = lax.rsqrt(ss * (1.0 / QK_HEAD) + RMS_EPS) * scale
            q_ref[rows, o:o + QK_NOPE] = (qn * r * qg_n).astype(BF16)
            q_ref[rows, o + QK_NOPE:o + QK_PAD] = (qr * r * q_tab).astype(BF16)
            knh = kn[:, h * QK_NOPE:(h + 1) * QK_NOPE]
            rk = lax.rsqrt((jnp.sum(knh * knh, axis=1, keepdims=True) + kpe_ss) * (1.0 / QK_HEAD) + RMS_EPS)
            k_ref[rows, o:o + QK_NOPE] = (knh * rk * kg_n).astype(BF16)
            k_ref[rows, o + QK_NOPE:o + QK_PAD] = (kr * rk).astype(BF16)
            if h % 2 == 1:
                yield

    _interleave([rows_chain(r0) for r0 in range(0, x_ref.shape[0], sub)], skew=2)


def _c_in(x2, p, rope_tab, tm, sub):
    n = x2.shape[0]
    tok = lambda w: pl.BlockSpec((tm, w), lambda i: (i, 0))
    names = ("norm_g", "wcq", "wckv", "wkpe", "wz", "qng", "kvng", "wuq", "wuk", "wuv", "qg", "kg")
    return pl.pallas_call(
        functools.partial(_c_in_kernel, sub=sub),
        grid=(n // tm,),
        in_specs=[tok(D_MODEL)] + [_const_spec(p[k].shape) for k in names] + [tok(LANES)],
        out_specs=[tok(C_HEADS * QK_PAD), tok(C_HEADS * QK_PAD), tok(C_HEADS * V_PAD), tok(D_MODEL)],
        out_shape=[jax.ShapeDtypeStruct((n, C_HEADS * QK_PAD), BF16),
                   jax.ShapeDtypeStruct((n, C_HEADS * QK_PAD), BF16),
                   jax.ShapeDtypeStruct((n, C_HEADS * V_PAD), BF16),
                   jax.ShapeDtypeStruct((n, D_MODEL), BF16)],
        compiler_params=_params(("parallel",)),
        name="c_in_proj",
    )(x2, *[p[k] for k in names], rope_tab)


def _attn_kernel(q_ref, k_ref, v_ref, o_ref, *, tq, heads):
    seq = q_ref.shape[0]
    nq = seq // tq
    rc = lax.broadcasted_iota(jnp.int32, (tq, tq), 0) // CHUNK
    cc = lax.broadcasted_iota(jnp.int32, (tq, tq), 1) // CHUNK
    mask = cc <= rc

    def row_max(blocks):
        m = blocks[0]
        for blk in blocks[1:]:
            m = jnp.maximum(m, blk)
        return jnp.max(m, axis=1, keepdims=True)

    def head_chain(h):
        qc = slice(h * QK_PAD, (h + 1) * QK_PAD)
        vc = slice(h * V_PAD, (h + 1) * V_PAD)

        def scores(qi, c):
            s = _dot_nt(q_ref[qi * tq:(qi + 1) * tq, qc], k_ref[c * tq:(c + 1) * tq, qc])
            return jnp.where(mask, s, -jnp.inf) if c == qi else s

        order = list(range(nq - 1, -1, -1))
        cur = [scores(order[0], c) for c in range(order[0] + 1)]
        yield
        for pos, qi in enumerate(order):
            m = row_max(cur)
            nq_i = order[pos + 1] if pos + 1 < nq else None
            nxt = []
            pb = []
            for c in range(qi + 1):
                pb.append(jnp.exp(cur[c] - m).astype(BF16))
                if nq_i is not None and c <= nq_i:
                    nxt.append(scores(nq_i, c))
            p_all = pb[0] if len(pb) == 1 else jnp.concatenate(pb, axis=1)
            acc = _dot(p_all, v_ref[0:(qi + 1) * tq, vc])
            o_ref[qi * tq:(qi + 1) * tq, h * V_HEAD:(h + 1) * V_HEAD] = (
                acc[:, 0:V_HEAD] * (1.0 / acc[:, V_HEAD:V_HEAD + 1])).astype(BF16)
            cur = nxt
            yield

    _interleave([head_chain(h) for h in range(heads)])


def _attention(q, k, v, batch, seq, tq, heads):
    n = q.shape[0]
    return pl.pallas_call(
        functools.partial(_attn_kernel, tq=tq, heads=heads),
        grid=(batch, C_HEADS // heads),
        in_specs=[pl.BlockSpec((seq, heads * QK_PAD), lambda b, h: (b, h)),
                  pl.BlockSpec((seq, heads * QK_PAD), lambda b, h: (b, h)),
                  pl.BlockSpec((seq, heads * V_PAD), lambda b, h: (b, h))],
        out_specs=pl.BlockSpec((seq, heads * V_HEAD), lambda b, h: (b, h)),
        out_shape=jax.ShapeDtypeStruct((n, C_HEADS * V_HEAD), BF16),
        compiler_params=_params(("parallel", "parallel")),
        name="block_causal_attention",
    )(q, k, v)


def _c_out_kernel(x_ref, o_ref, sz_ref, wo_ref, out_ref):
    gated = (o_ref[...].astype(F32) * sz_ref[...].astype(F32)).astype(BF16)
    out_ref[...] = x_ref[...] + _dot(gated, wo_ref[...])


def _c_out(x2, o, sz, w_out, tm):
    n = x2.shape[0]
    tok = lambda w: pl.BlockSpec((tm, w), lambda i: (i, 0))
    return pl.pallas_call(
        _c_out_kernel,
        grid=(n // tm,),
        in_specs=[tok(D_MODEL), tok(D_MODEL), tok(D_MODEL), _const_spec(w_out.shape)],
        out_specs=tok(D_MODEL),
        out_shape=jax.ShapeDtypeStruct((n, D_MODEL), F32),
        compiler_params=_params(("parallel",)),
        name="c_out_proj",
    )(x2, o, sz, w_out)


def _layer_c(x2, p, rope_tab, batch, seq):
    q, k, v, sz = _c_in(x2, p, rope_tab, tm=512, sub=256)
    o = _attention(q, k, v, batch, seq, tq=256, heads=2)
    return _c_out(x2, o, sz, p["w_out"], tm=1024)


def _pad_rope_cols(a):
    half = QK_ROPE // 2
    x1, x2 = a[..., :half], a[..., half:]
    return jnp.concatenate([x1, x2, x2, x1], axis=-1)


def _pad_head_cols(a):
    return jnp.concatenate([a[..., :QK_NOPE], _pad_rope_cols(a[..., QK_NOPE:])], axis=-1)


def _prep_c(norm_g, w_in, q_norm_g, kv_norm_g, w_uq, w_ukv, qn_g, kn_g, w_out):
    c0, c1, c2 = Q_LORA, Q_LORA + KV_LORA, Q_LORA + KV_LORA + QK_ROPE
    wuq = _pad_head_cols(w_uq.reshape(Q_LORA, C_HEADS, QK_HEAD)).reshape(Q_LORA, C_HEADS * QK_PAD)
    wukv = w_ukv.reshape(KV_LORA, C_HEADS, QK_NOPE + V_HEAD)
    return dict(
        norm_g=norm_g[None, :],
        wcq=w_in[:, :c0].astype(BF16), wckv=w_in[:, c0:c1].astype(BF16),
        wkpe=_pad_rope_cols(w_in[:, c1:c2]).astype(BF16), wz=w_in[:, c2:].astype(BF16),
        qng=q_norm_g[None, :], kvng=kv_norm_g[None, :],
        wuq=wuq.astype(BF16),
        wuk=wukv[:, :, :QK_NOPE].reshape(KV_LORA, C_HEADS * QK_NOPE).astype(BF16),
        wuv=wukv[:, :, QK_NOPE:].reshape(KV_LORA, C_HEADS * V_HEAD).astype(BF16),
        qg=_pad_head_cols(qn_g)[None, :], kg=_pad_head_cols(kn_g)[None, :],
        w_out=w_out.astype(BF16))


def kernel(x, positions, a_norm_g, a_w_in, a_b_if, a_conv_w, a_conv_b, a_w_q, a_w_k, a_w_v, a_head_norm_g, a_skip, a_pool_w, a_pool_scale, a_w_out, c_norm_g, c_w_in, c_q_norm_g, c_kv_norm_g, c_w_uq, c_w_ukv, c_qn_g, c_kn_g, c_w_out):
    batch, seq, d = x.shape
    depth = a_norm_g.shape[0] + c_norm_g.shape[0]
    a_args = (a_norm_g, a_w_in, a_b_if, a_conv_w, a_conv_b, a_w_q, a_w_k, a_w_v, a_head_norm_g, a_skip,
              a_pool_w, a_pool_scale, a_w_out)
    c_args = (c_norm_g, c_w_in, c_q_norm_g, c_kv_norm_g, c_w_uq, c_w_ukv, c_qn_g, c_kn_g, c_w_out)
    rope_tab = _rope_table(positions)
    x2 = x.reshape(batch * seq, d)
    for layer in range(depth):
        j = layer // 2
        if layer % 2 == 0:
            x2 = _layer_a(x2, _prep_a(*[a[j] for a in a_args]), batch, seq)
        else:
            x2 = _layer_c(x2, _prep_c(*[a[j] for a in c_args]), rope_tab, batch, seq)
    return x2.reshape(batch, seq, d)
```

```python
import functools

import jax
import jax.numpy as jnp
from jax import lax
from jax.experimental import pallas as pl
from jax.experimental.pallas import tpu as pltpu

F32 = jnp.float32
BF16 = jnp.bfloat16

D_MODEL = 1024
RMS_EPS = 1e-6
CHUNK = 64
M_HEADS = 4
M_HEAD_DIM = 256
CONV_K = 4
M_CHUNK = 256
P_WINDOWS = (2, 4, 8, 16)
P_GROUP_DIM = 256
HIST = 16
CONV_HIST = 8
C_HEADS = 8
QK_NOPE = 128
QK_ROPE = 64
QK_HEAD = QK_NOPE + QK_ROPE
QK_PAD = 256
V_HEAD = 128
V_PAD = 256
Q_LORA = 384
KV_LORA = 256
ROPE_THETA = 10000.0
LANES = 128

A_MIX_TILE = 2 * M_CHUNK
A_OUT_TILE, A_OUT_SUB = 1024, 256
C_IN_TILE, C_IN_SUB = 1024, 256
ATTN_Q_BLOCK, ATTN_HEADS_PER_STEP = 256, 2
C_OUT_TILE = 1024

V7X_VMEM_BYTES = 64 * 1024 * 1024
VMEM_LIMIT = V7X_VMEM_BYTES * 7 // 8


def _params(sem):
    return pltpu.CompilerParams(dimension_semantics=sem, vmem_limit_bytes=VMEM_LIMIT)


def _const_spec(shape):
    nd = len(shape)
    return pl.BlockSpec(shape, lambda *_: (0,) * nd, pipeline_mode=pl.Buffered(1))


def _layer_spec(stacked, j):
    nd = stacked.ndim - 1
    return pl.BlockSpec((None,) + stacked.shape[1:], lambda *_: (j,) + (0,) * nd, pipeline_mode=pl.Buffered(1))


def _rms(x, g):
    return x * lax.rsqrt(jnp.mean(x * x, axis=-1, keepdims=True) + RMS_EPS) * g


def _sigmoid(x):
    return 0.5 * jnp.tanh(0.5 * x) + 0.5


def _silu(x):
    h = 0.5 * x
    return h * jnp.tanh(h) + h


def _log_sigmoid(x):
    return jnp.minimum(x, 0.0) - jnp.log1p(jnp.exp(-jnp.abs(x)))


def _dot(a, b):
    return jnp.dot(a, b, preferred_element_type=F32)


def _dot_nt(a, b):
    return lax.dot_general(a, b, (((1,), (1,)), ((), ())), preferred_element_type=F32)


def _dot_tn(a, b):
    return lax.dot_general(a, b, (((0,), (0,)), ((), ())), preferred_element_type=F32)


def _interleave(chains, skew=0):
    live = dict(enumerate(chains))
    tick = 0
    while live:
        for i in sorted(live):
            if tick >= i * skew:
                try:
                    next(live[i])
                except StopIteration:
                    del live[i]
        tick += 1


def _rope_table_kernel(pos_ref, invf_ref, tab_ref):
    half = QK_ROPE // 2
    groups = LANES // half
    ang = pos_ref[...] * invf_ref[...]
    cos, sin = jnp.cos(ang), jnp.sin(ang)
    group = lax.broadcasted_iota(jnp.int32, ang.shape, 1) // half
    rolled = {0: (cos, sin)}
    for sh in range(half, LANES, half):
        rolled[sh] = (pltpu.roll(cos, sh, axis=1), pltpu.roll(sin, sh, axis=1))
    for k in range(groups):
        at = [rolled[((g - k) * half) % LANES] for g in range(groups)]
        tab_ref[k] = jnp.where(group == 0, at[0][0],
                               jnp.where(group == 1, at[1][0],
                                         jnp.where(group == 2, -at[2][1], at[3][1])))


def _rope_table(positions):
    n = positions.size
    half = QK_ROPE // 2
    per_row = LANES // half
    rows = n // per_row
    inv_freq = ROPE_THETA ** (-jnp.arange(0, QK_ROPE, 2, dtype=F32) / QK_ROPE)
    invf = jnp.tile(inv_freq, per_row)[None, :]
    pos = positions.reshape(per_row, rows).astype(F32).T
    pos = jnp.broadcast_to(pos[:, :, None], (rows, per_row, half)).reshape(rows, LANES)
    tr = min(rows, 2048)
    tab = pl.pallas_call(
        _rope_table_kernel,
        grid=(rows // tr,),
        in_specs=[pl.BlockSpec((tr, LANES), lambda i: (i, 0)), _const_spec((1, LANES))],
        out_specs=pl.BlockSpec((per_row, tr, LANES), lambda i: (0, i, 0)),
        out_shape=jax.ShapeDtypeStruct((per_row, rows, LANES), F32),
        compiler_params=_params(("parallel",)),
        name="rope_table",
    )(pos, invf)
    return tab.reshape(n, LANES)


def _a_mix_kernel(x_ref, g_ref, w3_ref, wg_ref, wgt_ref, bg_ref, bgt_ref, cw_ref, cb_ref,
                  wq_ref, wk_ref, wv_ref, hng_ref, skip_ref, ya_ref,
                  c_scr, n_scr, m_scr, hist_scr):
    t = M_CHUNK
    mw = M_HEADS * M_HEAD_DIM

    @pl.when(pl.program_id(1) == 0)
    def _():
        c_scr[...] = jnp.zeros_like(c_scr)
        n_scr[...] = jnp.zeros_like(n_scr)
        m_scr[...] = jnp.zeros_like(m_scr)
        hist_scr[...] = jnp.zeros_like(hist_scr)

    ri = lax.broadcasted_iota(jnp.int32, (t, t), 0)
    ci = lax.broadcasted_iota(jnp.int32, (t, t), 1)
    tril = ci <= ri
    triu = ri <= ci
    neg_inf = -jnp.inf

    def chunk_inputs(rows):
        yb = _rms(x_ref[rows, :], g_ref[...]).astype(BF16)
        graw = _dot(yb, wg_ref[...]) + bg_ref[...]
        lane = lax.broadcasted_iota(jnp.int32, graw.shape, 1)
        gcol = jnp.where(lane >= M_HEADS, _log_sigmoid(graw), graw)
        grawt = _dot_nt(wgt_ref[...], yb) + bgt_ref[...]
        row = lax.broadcasted_iota(jnp.int32, grawt.shape, 0)
        grow = jnp.where(row >= M_HEADS, _log_sigmoid(grawt), grawt)
        return yb, gcol, grow

    row_slices = [slice(r0, r0 + t) for r0 in range(0, x_ref.shape[0], t)]
    inputs = [chunk_inputs(rows) for rows in row_slices]

    def head_chain(h):
        hs = slice(h * M_HEAD_DIM, (h + 1) * M_HEAD_DIM)
        for rows, (yb, gcol, grow) in zip(row_slices, inputs):
            i_row = grow[h:h + 1, :]
            lf_row = grow[M_HEADS + h:M_HEADS + h + 1, :]
            i_col = gcol[:, h:h + 1]
            lf_col = gcol[:, M_HEADS + h:M_HEADS + h + 1]
            b_col = jnp.sum(jnp.where(tril, lf_row, 0.0), axis=1, keepdims=True)
            b_row = jnp.sum(jnp.where(triu, lf_col, 0.0), axis=0, keepdims=True)
            a_row = i_row - b_row
            a_col = i_col - b_col
            m_old = m_scr[h][:, 0:1]
            big_m = jnp.maximum(jnp.max(jnp.where(tril, a_row, neg_inf), axis=1, keepdims=True), m_old)
            decay = jnp.exp(jnp.where(tril, a_row - big_m, neg_inf))
            w_inter = jnp.exp(m_old - big_m)
            m_last = jnp.maximum(jnp.max(a_row, axis=1, keepdims=True), m_old)
            b_last = jnp.sum(lf_row, axis=1, keepdims=True)
            w_state = jnp.exp(a_col - m_last)
            wc = jnp.exp(m_old - m_last)
            inv_floor = jnp.exp(-(b_col + big_m))
            m_scr[h] = jnp.broadcast_to(b_last + m_last, (1, LANES))
            yield
            xm = _dot(yb, w3_ref[:, hs])
            yield
            xe = jnp.concatenate([hist_scr[:, hs], xm], axis=0)
            hist_scr[:, hs] = xm[t - CONV_HIST:, :]
            x1 = pltpu.roll(xe, 1, axis=0)
            u = cw_ref[3:4, hs] * xe + cw_ref[2:3, hs] * x1
            v2 = pltpu.roll(cw_ref[1:2, hs] * xe + cw_ref[0:1, hs] * x1, 2, axis=0)
            conv = cb_ref[:, hs] + (u + v2)[CONV_HIST:, :]
            xc = _silu(conv)
            xcb = xc.astype(BF16)
            yield
            q = _dot(xcb, wq_ref[h])
            k = _dot(xcb, wk_ref[h])
            v = _dot(xm.astype(BF16), wv_ref[h])
            qb, vb = q.astype(BF16), v.astype(BF16)
            yield
            s = _dot_nt(qb, k.astype(BF16)) * decay
            n_row = n_scr[h]
            den = (jnp.sum(s, axis=1, keepdims=True)
                   + w_inter * jnp.sum(q * n_row, axis=1, keepdims=True))
            yield
            num = _dot(s.astype(BF16), vb) + w_inter * _dot(qb, c_scr[h].astype(BF16))
            hh = num * (1.0 / jnp.maximum(jnp.abs(den), inv_floor))
            kw = k * w_state
            c_scr[h] = wc * c_scr[h] + _dot_tn(kw.astype(BF16), vb)
            n_scr[h] = wc * n_row + jnp.sum(kw, axis=0, keepdims=True)
            yield
            om = _dot(yb, w3_ref[:, mw + h * M_HEAD_DIM:mw + (h + 1) * M_HEAD_DIM])
            zm = _dot(yb, w3_ref[:, 2 * mw + h * M_HEAD_DIM:2 * mw + (h + 1) * M_HEAD_DIM])
            yield
            hm = _rms(hh * _sigmoid(om), hng_ref[:, hs])
            ya_ref[rows, hs] = ((hm + skip_ref[:, hs] * xc) * _silu(zm)).astype(BF16)

    _interleave([head_chain(h) for h in range(M_HEADS)])


def _a_mix(x2, p, j, batch, seq):
    n = x2.shape[0]
    t = A_MIX_TILE
    spt = seq // t
    tok = lambda w: pl.BlockSpec((t, w), lambda b, s: (b * spt + s, 0))
    names = ("norm_g", "w3", "wg", "wgt", "bg", "bgt", "conv_w", "conv_b", "wq", "wk", "wv", "hng", "skip")
    return pl.pallas_call(
        _a_mix_kernel,
        grid=(batch, spt),
        in_specs=[tok(D_MODEL)] + [_layer_spec(p[k], j) for k in names],
        out_specs=tok(D_MODEL),
        out_shape=jax.ShapeDtypeStruct((n, D_MODEL), BF16),
        scratch_shapes=[pltpu.VMEM((M_HEADS, M_HEAD_DIM, M_HEAD_DIM), F32),
                        pltpu.VMEM((M_HEADS, 1, M_HEAD_DIM), F32),
                        pltpu.VMEM((M_HEADS, 1, LANES), F32),
                        pltpu.VMEM((CONV_HIST, D_MODEL), F32)],
        compiler_params=_params(("parallel", "arbitrary")),
        name="a_mlstm",
    )(x2, *[p[k] for k in names])


def _a_out_kernel(x_ref, ya_ref, g_ref, wp_ref, pw_ref, ps_ref, wo_ref, o_ref, hist_scr, *, sub):
    tm = x_ref.shape[0]
    si = pl.program_id(1)

    @pl.when(si == 0)
    def _():
        hist_scr[...] = jnp.zeros_like(hist_scr)

    pw_all = len(P_WINDOWS) * P_GROUP_DIM
    row_slices = [slice(r0, r0 + sub) for r0 in range(0, tm, sub)]
    parts = [[None] * (len(P_WINDOWS) + 1) for _ in row_slices]
    xs = [x_ref[rows, :] for rows in row_slices]
    ybs = [_rms(x, g_ref[...]).astype(BF16) for x in xs]

    def skip_chain():
        for i, rows in enumerate(row_slices):
            parts[i][0] = xs[i] + _dot(ya_ref[rows, :], wo_ref[0:D_MODEL, :])
            yield

    def group_chain(g, w):
        gs = slice(g * P_GROUP_DIM, (g + 1) * P_GROUP_DIM)
        for i, rows in enumerate(row_slices):
            yb = ybs[i]
            tpos = si * tm + rows.start + lax.broadcasted_iota(jnp.int32, (sub, 1), 0)
            xp = _dot(yb, wp_ref[:, gs])
            zp = _dot(yb, wp_ref[:, pw_all + g * P_GROUP_DIM:pw_all + (g + 1) * P_GROUP_DIM])
            yield
            xe = jnp.concatenate([hist_scr[:, gs], xp], axis=0)
            hist_scr[:, gs] = xp[sub - HIST:, :]
            tot = xe
            sh = 1
            while sh < w:
                tot = tot + pltpu.roll(tot, sh, axis=0)
                sh *= 2
            inv_cnt = 1.0 / jnp.minimum(tpos + 1, w).astype(F32)
            mix = (tot[HIST:, :] * inv_cnt - xp).astype(BF16)
            gate = ps_ref[:, gs] * _silu(zp)
            yield
            yb_g = (_dot(mix, pw_ref[g]) * gate).astype(BF16)
            yield
            parts[i][g + 1] = _dot(yb_g, wo_ref[D_MODEL + g * P_GROUP_DIM:D_MODEL + (g + 1) * P_GROUP_DIM, :])
            yield

    _interleave([skip_chain()] + [group_chain(g, w) for g, w in enumerate(P_WINDOWS)])
    for i, rows in enumerate(row_slices):
        acc = parts[i][0]
        for part in parts[i][1:]:
            acc = acc + part
        o_ref[rows, :] = acc


def _a_out(x2, ya, p, j, batch, seq, tm, sub):
    n = x2.shape[0]
    spt = seq // tm
    tok = lambda w: pl.BlockSpec((tm, w), lambda b, s: (b * spt + s, 0))
    names = ("norm_g", "wp", "pool_w", "pool_scale", "w_out")
    return pl.pallas_call(
        functools.partial(_a_out_kernel, sub=sub),
        grid=(batch, spt),
        in_specs=[tok(D_MODEL), tok(D_MODEL)] + [_layer_spec(p[k], j) for k in names],
        out_specs=tok(D_MODEL),
        out_shape=jax.ShapeDtypeStruct((n, D_MODEL), F32),
        scratch_shapes=[pltpu.VMEM((HIST, D_MODEL), F32)],
        compiler_params=_params(("parallel", "arbitrary")),
        name="a_pool_out_proj",
    )(x2, ya, *[p[k] for k in names])


def _layer_a(x2, p, j, batch, seq):
    ya = _a_mix(x2, p, j, batch, seq)
    return _a_out(x2, ya, p, j, batch, seq, tm=A_OUT_TILE, sub=A_OUT_SUB)


def _prep_a(norm_g, w_in, b_if, conv_w, conv_b, w_q, w_k, w_v, head_norm_g, skip, pool_w, pool_scale, w_out):
    mw = M_HEADS * M_HEAD_DIM
    g0 = 2 * mw
    g1 = g0 + 2 * M_HEADS
    z1 = g1 + mw
    wg = w_in[..., g0:g1]
    pad = [(0, 0)] * (wg.ndim - 1) + [(0, LANES - 2 * M_HEADS)]
    row = lambda a: a[:, None, :]
    return dict(
        norm_g=row(norm_g),
        w3=jnp.concatenate([w_in[..., :g0], w_in[..., g1:z1]], axis=-1).astype(BF16),
        wp=w_in[..., z1:].astype(BF16),
        wg=jnp.pad(wg, pad).astype(BF16), wgt=jnp.swapaxes(wg, -1, -2).astype(BF16),
        bg=row(jnp.pad(b_if, pad[1:])), bgt=b_if[:, :, None],
        conv_w=conv_w, conv_b=row(conv_b),
        wq=w_q.astype(BF16), wk=(w_k * (M_HEAD_DIM ** -0.5)).astype(BF16), wv=w_v.astype(BF16),
        hng=row(head_norm_g), skip=row(skip),
        pool_w=pool_w.astype(BF16), pool_scale=row(pool_scale), w_out=w_out.astype(BF16))


def _c_in_kernel(x_ref, g_ref, wcq_ref, wckv_ref, wkpe_ref, wz_ref, qng_ref, kvng_ref,
                 wuq_ref, wuk_ref, wuv_ref, qg_ref, kg_ref, tab_ref,
                 q_ref, k_ref, v_ref, sz_ref, *, sub):
    qg_n, qg_r = qg_ref[:, 0:QK_NOPE], qg_ref[:, QK_NOPE:QK_PAD]
    kg_n, kg_r = kg_ref[:, 0:QK_NOPE], kg_ref[:, QK_NOPE:QK_PAD]
    scale = QK_HEAD ** -0.5

    def rows_chain(r0):
        rows = slice(r0, r0 + sub)
        yb = _rms(x_ref[rows, :], g_ref[...]).astype(BF16)
        yield
        z = _dot(yb, wz_ref[...])
        cq = _dot(yb, wcq_ref[...])
        ckv = _dot(yb, wckv_ref[...])
        kpe = _dot(yb, wkpe_ref[...])
        yield
        cqn = _rms(cq, qng_ref[...]).astype(BF16)
        ckvn = _rms(ckv, kvng_ref[...]).astype(BF16)
        sz_ref[rows, :] = _silu(z).astype(BF16)
        tab = tab_ref[rows, :]
        q_tab = tab * qg_r
        kpe_ss = 0.5 * jnp.sum(kpe * kpe, axis=1, keepdims=True)
        kp = kpe * (tab * kg_r)
        kr = kp + pltpu.roll(kp, LANES // 2, axis=1)
        yield
        q = _dot(cqn, wuq_ref[...])
        kn = _dot(ckvn, wuk_ref[...])
        v = _dot(ckvn, wuv_ref[...]).astype(BF16)
        yield
        ones_col = (lax.broadcasted_iota(jnp.int32, (sub, V_PAD - V_HEAD), 1) == 0).astype(BF16)
        for h in range(C_HEADS):
            v_ref[rows, h * V_PAD:h * V_PAD + V_HEAD] = v[:, h * V_HEAD:(h + 1) * V_HEAD]
            v_ref[rows, h * V_PAD + V_HEAD:(h + 1) * V_PAD] = ones_col
            o = h * QK_PAD
            qn = q[:, o:o + QK_NOPE]
            qr = q[:, o + QK_NOPE:o + QK_PAD]
            ss = jnp.sum(qn * qn + 0.5 * (qr * qr), axis=1, keepdims=True)
            r = lax.rsqrt(ss * (1.0 / QK_HEAD) + RMS_EPS) * scale
            q_ref[rows, o:o + QK_NOPE] = (qn * r * qg_n).astype(BF16)
            q_ref[rows, o + QK_NOPE:o + QK_PAD] = (qr * r * q_tab).astype(BF16)
            knh = kn[:, h * QK_NOPE:(h + 1) * QK_NOPE]
            rk = lax.rsqrt((jnp.sum(knh * knh, axis=1, keepdims=True) + kpe_ss) * (1.0 / QK_HEAD) + RMS_EPS)
            k_ref[rows, o:o + QK_NOPE] = (knh * rk * kg_n).astype(BF16)
            k_ref[rows, o + QK_NOPE:o + QK_PAD] = (kr * rk).astype(BF16)
            if h % 2 == 1:
                yield

    _interleave([rows_chain(r0) for r0 in range(0, x_ref.shape[0], sub)], skew=2)


def _c_in(x2, p, j, rope_tab, tm, sub):
    n = x2.shape[0]
    tok = lambda w: pl.BlockSpec((tm, w), lambda i: (i, 0))
    names = ("norm_g", "wcq", "wckv", "wkpe", "wz", "qng", "kvng", "wuq", "wuk", "wuv", "qg", "kg")
    return pl.pallas_call(
        functools.partial(_c_in_kernel, sub=sub),
        grid=(n // tm,),
        in_specs=[tok(D_MODEL)] + [_layer_spec(p[k], j) for k in names] + [tok(LANES)],
        out_specs=[tok(C_HEADS * QK_PAD), tok(C_HEADS * QK_PAD), tok(C_HEADS * V_PAD), tok(D_MODEL)],
        out_shape=[jax.ShapeDtypeStruct((n, C_HEADS * QK_PAD), BF16),
                   jax.ShapeDtypeStruct((n, C_HEADS * QK_PAD), BF16),
                   jax.ShapeDtypeStruct((n, C_HEADS * V_PAD), BF16),
                   jax.ShapeDtypeStruct((n, D_MODEL), BF16)],
        compiler_params=_params(("parallel",)),
        name="c_in_proj",
    )(x2, *[p[k] for k in names], rope_tab)


def _attn_kernel(q_ref, k_ref, v_ref, o_ref, *, tq, heads):
    seq = q_ref.shape[0]
    nq = seq // tq
    rc = lax.broadcasted_iota(jnp.int32, (tq, tq), 0) // CHUNK
    cc = lax.broadcasted_iota(jnp.int32, (tq, tq), 1) // CHUNK
    mask = cc <= rc

    def row_max(blocks):
        m = blocks[0]
        for blk in blocks[1:]:
            m = jnp.maximum(m, blk)
        return jnp.max(m, axis=1, keepdims=True)

    def head_chain(h):
        qc = slice(h * QK_PAD, (h + 1) * QK_PAD)
        vc = slice(h * V_PAD, (h + 1) * V_PAD)

        def scores(qi, c):
            s = _dot_nt(q_ref[qi * tq:(qi + 1) * tq, qc], k_ref[c * tq:(c + 1) * tq, qc])
            return jnp.where(mask, s, -jnp.inf) if c == qi else s

        order = list(range(nq - 1, -1, -1))
        cur = [scores(order[0], c) for c in range(order[0] + 1)]
        yield
        for pos, qi in enumerate(order):
            m = row_max(cur)
            nq_i = order[pos + 1] if pos + 1 < nq else None
            nxt = []
            pb = []
            for c in range(qi + 1):
                pb.append(jnp.exp(cur[c] - m).astype(BF16))
                if nq_i is not None and c <= nq_i:
                    nxt.append(scores(nq_i, c))
            p_all = pb[0] if len(pb) == 1 else jnp.concatenate(pb, axis=1)
            acc = _dot(p_all, v_ref[0:(qi + 1) * tq, vc])
            o_ref[qi * tq:(qi + 1) * tq, h * V_HEAD:(h + 1) * V_HEAD] = (
                acc[:, 0:V_HEAD] * (1.0 / acc[:, V_HEAD:V_HEAD + 1])).astype(BF16)
            cur = nxt
            yield

    _interleave([head_chain(h) for h in range(heads)])


def _attention(q, k, v, batch, seq, tq, heads):
    n = q.shape[0]
    return pl.pallas_call(
        functools.partial(_attn_kernel, tq=tq, heads=heads),
        grid=(batch, C_HEADS // heads),
        in_specs=[pl.BlockSpec((seq, heads * QK_PAD), lambda b, h: (b, h)),
                  pl.BlockSpec((seq, heads * QK_PAD), lambda b, h: (b, h)),
                  pl.BlockSpec((seq, heads * V_PAD), lambda b, h: (b, h))],
        out_specs=pl.BlockSpec((seq, heads * V_HEAD), lambda b, h: (b, h)),
        out_shape=jax.ShapeDtypeStruct((n, C_HEADS * V_HEAD), BF16),
        compiler_params=_params(("parallel", "parallel")),
        name="block_causal_attention",
    )(q, k, v)


def _c_out_kernel(x_ref, o_ref, sz_ref, wo_ref, out_ref):
    gated = (o_ref[...].astype(F32) * sz_ref[...].astype(F32)).astype(BF16)
    out_ref[...] = x_ref[...] + _dot(gated, wo_ref[...])


def _c_out(x2, o, sz, w_out, j, tm):
    n = x2.shape[0]
    tok = lambda w: pl.BlockSpec((tm, w), lambda i: (i, 0))
    return pl.pallas_call(
        _c_out_kernel,
        grid=(n // tm,),
        in_specs=[tok(D_MODEL), tok(D_MODEL), tok(D_MODEL), _layer_spec(w_out, j)],
        out_specs=tok(D_MODEL),
        out_shape=jax.ShapeDtypeStruct((n, D_MODEL), F32),
        compiler_params=_params(("parallel",)),
        name="c_out_proj",
    )(x2, o, sz, w_out)


def _layer_c(x2, p, j, rope_tab, batch, seq):
    q, k, v, sz = _c_in(x2, p, j, rope_tab, tm=C_IN_TILE, sub=C_IN_SUB)
    o = _attention(q, k, v, batch, seq, tq=ATTN_Q_BLOCK, heads=ATTN_HEADS_PER_STEP)
    return _c_out(x2, o, sz, p["w_out"], j, tm=C_OUT_TILE)


def _pad_rope_cols(a):
    half = QK_ROPE // 2
    x1, x2 = a[..., :half], a[..., half:]
    return jnp.concatenate([x1, x2, x2, x1], axis=-1)


def _pad_head_cols(a):
    return jnp.concatenate([a[..., :QK_NOPE], _pad_rope_cols(a[..., QK_NOPE:])], axis=-1)


def _prep_c(norm_g, w_in, q_norm_g, kv_norm_g, w_uq, w_ukv, qn_g, kn_g, w_out):
    nl = w_in.shape[0]
    c0, c1, c2 = Q_LORA, Q_LORA + KV_LORA, Q_LORA + KV_LORA + QK_ROPE
    wuq = _pad_head_cols(w_uq.reshape(nl, Q_LORA, C_HEADS, QK_HEAD)).reshape(nl, Q_LORA, C_HEADS * QK_PAD)
    wukv = w_ukv.reshape(nl, KV_LORA, C_HEADS, QK_NOPE + V_HEAD)
    row = lambda a: a[:, None, :]
    return dict(
        norm_g=row(norm_g),
        wcq=w_in[..., :c0].astype(BF16), wckv=w_in[..., c0:c1].astype(BF16),
        wkpe=_pad_rope_cols(w_in[..., c1:c2]).astype(BF16), wz=w_in[..., c2:].astype(BF16),
        qng=row(q_norm_g), kvng=row(kv_norm_g),
        wuq=wuq.astype(BF16),
        wuk=wukv[..., :QK_NOPE].reshape(nl, KV_LORA, C_HEADS * QK_NOPE).astype(BF16),
        wuv=wukv[..., QK_NOPE:].reshape(nl, KV_LORA, C_HEADS * V_HEAD).astype(BF16),
        qg=row(_pad_head_cols(qn_g)), kg=row(_pad_head_cols(kn_g)),
        w_out=w_out.astype(BF16))


def kernel(x, positions, a_norm_g, a_w_in, a_b_if, a_conv_w, a_conv_b, a_w_q, a_w_k, a_w_v, a_head_norm_g, a_skip, a_pool_w, a_pool_scale, a_w_out, c_norm_g, c_w_in, c_q_norm_g, c_kv_norm_g, c_w_uq, c_w_ukv, c_qn_g, c_kn_g, c_w_out):
    batch, seq, d = x.shape
    depth = a_norm_g.shape[0] + c_norm_g.shape[0]
    a_args = (a_norm_g, a_w_in, a_b_if, a_conv_w, a_conv_b, a_w_q, a_w_k, a_w_v, a_head_norm_g, a_skip,
              a_pool_w, a_pool_scale, a_w_out)
    c_args = (c_norm_g, c_w_in, c_q_norm_g, c_kv_norm_g, c_w_uq, c_w_ukv, c_qn_g, c_kn_g, c_w_out)
    rope_tab = _rope_table(positions)
    pa, pc = _prep_a(*a_args), _prep_c(*c_args)
    x2 = x.reshape(batch * seq, d)
    for layer in range(depth):
        j = layer // 2
        if layer % 2 == 0:
            x2 = _layer_a(x2, pa, j, batch, seq)
        else:
            x2 = _layer_c(x2, pc, j, rope_tab, batch, seq)
    return x2.reshape(batch, seq, d)
```

```python
import functools

import jax
import jax.numpy as jnp
from jax import lax
from jax.experimental import pallas as pl
from jax.experimental.pallas import tpu as pltpu

F32 = jnp.float32
BF16 = jnp.bfloat16

D_MODEL = 1024
RMS_EPS = 1e-6
CHUNK = 64
M_HEADS = 4
M_HEAD_DIM = 256
CONV_K = 4
M_CHUNK = 256
P_WINDOWS = (2, 4, 8, 16)
P_GROUP_DIM = 256
HIST = 16
CONV_HIST = 8
C_HEADS = 8
QK_NOPE = 128
QK_ROPE = 64
QK_HEAD = QK_NOPE + QK_ROPE
QK_PAD = 256
V_HEAD = 128
V_PAD = 256
Q_LORA = 384
KV_LORA = 256
ROPE_THETA = 10000.0
LANES = 128

A_MIX_TILE = 2 * M_CHUNK
A_OUT_TILE, A_OUT_SUB = 1024, 256
C_IN_TILE, C_IN_SUB = 1024, 256
ATTN_Q_BLOCK, ATTN_HEADS_PER_STEP = 256, 2
C_OUT_TILE = 1024

V7X_VMEM_BYTES = 64 * 1024 * 1024
VMEM_LIMIT = V7X_VMEM_BYTES * 7 // 8


def _params(sem):
    return pltpu.CompilerParams(dimension_semantics=sem, vmem_limit_bytes=VMEM_LIMIT)


def _const_spec(shape):
    nd = len(shape)
    return pl.BlockSpec(shape, lambda *_: (0,) * nd, pipeline_mode=pl.Buffered(1))


def _layer_spec(stacked, j):
    nd = stacked.ndim - 1
    return pl.BlockSpec((None,) + stacked.shape[1:], lambda *_: (j,) + (0,) * nd, pipeline_mode=pl.Buffered(1))


def _rms(x, g):
    return x * lax.rsqrt(jnp.mean(x * x, axis=-1, keepdims=True) + RMS_EPS) * g


def _sigmoid(x):
    return 0.5 * jnp.tanh(0.5 * x) + 0.5


def _silu(x):
    h = 0.5 * x
    return h * jnp.tanh(h) + h


def _log_sigmoid(x):
    return jnp.minimum(x, 0.0) - jnp.log1p(jnp.exp(-jnp.abs(x)))


def _dot(a, b):
    return jnp.dot(a, b, preferred_element_type=F32)


def _dot_nt(a, b):
    return lax.dot_general(a, b, (((1,), (1,)), ((), ())), preferred_element_type=F32)


def _dot_tn(a, b):
    return lax.dot_general(a, b, (((0,), (0,)), ((), ())), preferred_element_type=F32)


def _interleave(chains, skew=0):
    live = dict(enumerate(chains))
    tick = 0
    while live:
        for i in sorted(live):
            if tick >= i * skew:
                try:
                    next(live[i])
                except StopIteration:
                    del live[i]
        tick += 1


def _rope_table_kernel(pos_ref, invf_ref, tab_ref):
    half = QK_ROPE // 2
    groups = LANES // half
    ang = pos_ref[...] * invf_ref[...]
    cos, sin = jnp.cos(ang), jnp.sin(ang)
    group = lax.broadcasted_iota(jnp.int32, ang.shape, 1) // half
    rolled = {0: (cos, sin)}
    for sh in range(half, LANES, half):
        rolled[sh] = (pltpu.roll(cos, sh, axis=1), pltpu.roll(sin, sh, axis=1))
    for k in range(groups):
        at = [rolled[((g - k) * half) % LANES] for g in range(groups)]
        tab_ref[k] = jnp.where(group == 0, at[0][0],
                               jnp.where(group == 1, at[1][0],
                                         jnp.where(group == 2, -at[2][1], at[3][1])))


def _rope_table(positions):
    n = positions.size
    half = QK_ROPE // 2
    per_row = LANES // half
    rows = n // per_row
    inv_freq = ROPE_THETA ** (-jnp.arange(0, QK_ROPE, 2, dtype=F32) / QK_ROPE)
    invf = jnp.tile(inv_freq, per_row)[None, :]
    pos = positions.reshape(per_row, rows).astype(F32).T
    pos = jnp.broadcast_to(pos[:, :, None], (rows, per_row, half)).reshape(rows, LANES)
    tr = min(rows, 2048)
    tab = pl.pallas_call(
        _rope_table_kernel,
        grid=(rows // tr,),
        in_specs=[pl.BlockSpec((tr, LANES), lambda i: (i, 0)), _const_spec((1, LANES))],
        out_specs=pl.BlockSpec((per_row, tr, LANES), lambda i: (0, i, 0)),
        out_shape=jax.ShapeDtypeStruct((per_row, rows, LANES), F32),
        compiler_params=_params(("parallel",)),
        name="rope_table",
    )(pos, invf)
    return tab.reshape(n, LANES)


def _a_mix_kernel(x_ref, g_ref, w3_ref, wg_ref, wgt_ref, bg_ref, bgt_ref, cw_ref, cb_ref,
                  wq_ref, wk_ref, wv_ref, hng_ref, skip_ref, ya_ref,
                  c_scr, n_scr, m_scr, hist_scr):
    t = M_CHUNK
    mw = M_HEADS * M_HEAD_DIM

    @pl.when(pl.program_id(1) == 0)
    def _():
        c_scr[...] = jnp.zeros_like(c_scr)
        n_scr[...] = jnp.zeros_like(n_scr)
        m_scr[...] = jnp.zeros_like(m_scr)
        hist_scr[...] = jnp.zeros_like(hist_scr)

    ri = lax.broadcasted_iota(jnp.int32, (t, t), 0)
    ci = lax.broadcasted_iota(jnp.int32, (t, t), 1)
    tril = ci <= ri
    triu = ri <= ci
    neg_inf = -jnp.inf

    def chunk_inputs(rows):
        yb = _rms(x_ref[rows, :], g_ref[...]).astype(BF16)
        graw = _dot(yb, wg_ref[...]) + bg_ref[...]
        lane = lax.broadcasted_iota(jnp.int32, graw.shape, 1)
        gcol = jnp.where(lane >= M_HEADS, _log_sigmoid(graw), graw)
        grawt = _dot_nt(wgt_ref[...], yb) + bgt_ref[...]
        row = lax.broadcasted_iota(jnp.int32, grawt.shape, 0)
        grow = jnp.where(row >= M_HEADS, _log_sigmoid(grawt), grawt)
        return yb, gcol, grow

    row_slices = [slice(r0, r0 + t) for r0 in range(0, x_ref.shape[0], t)]
    inputs = [chunk_inputs(rows) for rows in row_slices]

    def head_chain(h):
        hs = slice(h * M_HEAD_DIM, (h + 1) * M_HEAD_DIM)
        for rows, (yb, gcol, grow) in zip(row_slices, inputs):
            i_row = grow[h:h + 1, :]
            lf_row = grow[M_HEADS + h:M_HEADS + h + 1, :]
            i_col = gcol[:, h:h + 1]
            lf_col = gcol[:, M_HEADS + h:M_HEADS + h + 1]
            b_col = jnp.sum(jnp.where(tril, lf_row, 0.0), axis=1, keepdims=True)
            b_row = jnp.sum(jnp.where(triu, lf_col, 0.0), axis=0, keepdims=True)
            a_row = i_row - b_row
            a_col = i_col - b_col
            m_old = m_scr[h][:, 0:1]
            big_m = jnp.maximum(jnp.max(jnp.where(tril, a_row, neg_inf), axis=1, keepdims=True), m_old)
            decay = jnp.exp(jnp.where(tril, a_row - big_m, neg_inf))
            w_inter = jnp.exp(m_old - big_m)
            m_last = jnp.maximum(jnp.max(a_row, axis=1, keepdims=True), m_old)
            b_last = jnp.sum(lf_row, axis=1, keepdims=True)
            w_state = jnp.exp(a_col - m_last)
            wc = jnp.exp(m_old - m_last)
            inv_floor = jnp.exp(-(b_col + big_m))
            m_scr[h] = jnp.broadcast_to(b_last + m_last, (1, LANES))
            yield
            xm = _dot(yb, w3_ref[:, hs])
            yield
            xe = jnp.concatenate([hist_scr[:, hs], xm], axis=0)
            hist_scr[:, hs] = xm[t - CONV_HIST:, :]
            x1 = pltpu.roll(xe, 1, axis=0)
            u = cw_ref[3:4, hs] * xe + cw_ref[2:3, hs] * x1
            v2 = pltpu.roll(cw_ref[1:2, hs] * xe + cw_ref[0:1, hs] * x1, 2, axis=0)
            conv = cb_ref[:, hs] + (u + v2)[CONV_HIST:, :]
            xc = _silu(conv)
            xcb = xc.astype(BF16)
            yield
            q = _dot(xcb, wq_ref[h])
            k = _dot(xcb, wk_ref[h])
            v = _dot(xm.astype(BF16), wv_ref[h])
            qb, vb = q.astype(BF16), v.astype(BF16)
            yield
            s = _dot_nt(qb, k.astype(BF16)) * decay
            n_row = n_scr[h]
            den = (jnp.sum(s, axis=1, keepdims=True)
                   + w_inter * jnp.sum(q * n_row, axis=1, keepdims=True))
            yield
            num = _dot(s.astype(BF16), vb) + w_inter * _dot(qb, c_scr[h].astype(BF16))
            hh = num * (1.0 / jnp.maximum(jnp.abs(den), inv_floor))
            kw = k * w_state
            c_scr[h] = wc * c_scr[h] + _dot_tn(kw.astype(BF16), vb)
            n_scr[h] = wc * n_row + jnp.sum(kw, axis=0, keepdims=True)
            yield
            om = _dot(yb, w3_ref[:, mw + h * M_HEAD_DIM:mw + (h + 1) * M_HEAD_DIM])
            zm = _dot(yb, w3_ref[:, 2 * mw + h * M_HEAD_DIM:2 * mw + (h + 1) * M_HEAD_DIM])
            yield
            hm = _rms(hh * _sigmoid(om), hng_ref[:, hs])
            ya_ref[rows, hs] = ((hm + skip_ref[:, hs] * xc) * _silu(zm)).astype(BF16)

    _interleave([head_chain(h) for h in range(M_HEADS)])


def _a_mix(x2, p, j, batch, seq):
    n = x2.shape[0]
    t = A_MIX_TILE
    spt = seq // t
    tok = lambda w: pl.BlockSpec((t, w), lambda b, s: (b * spt + s, 0))
    names = ("norm_g", "w3", "wg", "wgt", "bg", "bgt", "conv_w", "conv_b", "wq", "wk", "wv", "hng", "skip")
    return pl.pallas_call(
        _a_mix_kernel,
        grid=(batch, spt),
        in_specs=[tok(D_MODEL)] + [_layer_spec(p[k], j) for k in names],
        out_specs=tok(D_MODEL),
        out_shape=jax.ShapeDtypeStruct((n, D_MODEL), BF16),
        scratch_shapes=[pltpu.VMEM((M_HEADS, M_HEAD_DIM, M_HEAD_DIM), F32),
                        pltpu.VMEM((M_HEADS, 1, M_HEAD_DIM), F32),
                        pltpu.VMEM((M_HEADS, 1, LANES), F32),
                        pltpu.VMEM((CONV_HIST, D_MODEL), F32)],
        compiler_params=_params(("parallel", "arbitrary")),
        name="a_mlstm",
    )(x2, *[p[k] for k in names])


def _a_out_kernel(x_ref, ya_ref, g_ref, wp_ref, pw_ref, ps_ref, wo_ref, o_ref, hist_scr, *, sub):
    tm = x_ref.shape[0]
    si = pl.program_id(1)

    @pl.when(si == 0)
    def _():
        hist_scr[...] = jnp.zeros_like(hist_scr)

    pw_all = len(P_WINDOWS) * P_GROUP_DIM
    row_slices = [slice(r0, r0 + sub) for r0 in range(0, tm, sub)]
    parts = [[None] * (len(P_WINDOWS) + 1) for _ in row_slices]
    xs = [x_ref[rows, :] for rows in row_slices]
    ybs = [_rms(x, g_ref[...]).astype(BF16) for x in xs]

    def skip_chain():
        for i, rows in enumerate(row_slices):
            parts[i][0] = xs[i] + _dot(ya_ref[rows, :], wo_ref[0:D_MODEL, :])
            yield

    def group_chain(g, w):
        gs = slice(g * P_GROUP_DIM, (g + 1) * P_GROUP_DIM)
        for i, rows in enumerate(row_slices):
            yb = ybs[i]
            tpos = si * tm + rows.start + lax.broadcasted_iota(jnp.int32, (sub, 1), 0)
            xp = _dot(yb, wp_ref[:, gs])
            zp = _dot(yb, wp_ref[:, pw_all + g * P_GROUP_DIM:pw_all + (g + 1) * P_GROUP_DIM])
            yield
            xe = jnp.concatenate([hist_scr[:, gs], xp], axis=0)
            hist_scr[:, gs] = xp[sub - HIST:, :]
            tot = xe
            sh = 1
            while sh < w:
                tot = tot + pltpu.roll(tot, sh, axis=0)
                sh *= 2
            inv_cnt = 1.0 / jnp.minimum(tpos + 1, w).astype(F32)
            mix = (tot[HIST:, :] * inv_cnt - xp).astype(BF16)
            gate = ps_ref[:, gs] * _silu(zp)
            yield
            yb_g = (_dot(mix, pw_ref[g]) * gate).astype(BF16)
            yield
            parts[i][g + 1] = _dot(yb_g, wo_ref[D_MODEL + g * P_GROUP_DIM:D_MODEL + (g + 1) * P_GROUP_DIM, :])
            yield

    _interleave([skip_chain()] + [group_chain(g, w) for g, w in enumerate(P_WINDOWS)])
    for i, rows in enumerate(row_slices):
        acc = parts[i][0]
        for part in parts[i][1:]:
            acc = acc + part
        o_ref[rows, :] = acc


def _a_out(x2, ya, p, j, batch, seq, tm, sub):
    n = x2.shape[0]
    spt = seq // tm
    tok = lambda w: pl.BlockSpec((tm, w), lambda b, s: (b * spt + s, 0))
    names = ("norm_g", "wp", "pool_w", "pool_scale", "w_out")
    return pl.pallas_call(
        functools.partial(_a_out_kernel, sub=sub),
        grid=(batch, spt),
        in_specs=[tok(D_MODEL), tok(D_MODEL)] + [_layer_spec(p[k], j) for k in names],
        out_specs=tok(D_MODEL),
        out_shape=jax.ShapeDtypeStruct((n, D_MODEL), F32),
        scratch_shapes=[pltpu.VMEM((HIST, D_MODEL), F32)],
        compiler_params=_params(("parallel", "arbitrary")),
        name="a_pool_out_proj",
    )(x2, ya, *[p[k] for k in names])


def _layer_a(x2, p, j, batch, seq):
    ya = _a_mix(x2, p, j, batch, seq)
    return _a_out(x2, ya, p, j, batch, seq, tm=A_OUT_TILE, sub=A_OUT_SUB)


def _prep_a(norm_g, w_in, b_if, conv_w, conv_b, w_q, w_k, w_v, head_norm_g, skip, pool_w, pool_scale, w_out):
    mw = M_HEADS * M_HEAD_DIM
    g0 = 2 * mw
    g1 = g0 + 2 * M_HEADS
    z1 = g1 + mw
    wg = w_in[..., g0:g1]
    pad = [(0, 0)] * (wg.ndim - 1) + [(0, LANES - 2 * M_HEADS)]
    row = lambda a: a[:, None, :]
    return dict(
        norm_g=row(norm_g),
        w3=jnp.concatenate([w_in[..., :g0], w_in[..., g1:z1]], axis=-1).astype(BF16),
        wp=w_in[..., z1:].astype(BF16),
        wg=jnp.pad(wg, pad).astype(BF16), wgt=jnp.swapaxes(wg, -1, -2).astype(BF16),
        bg=row(jnp.pad(b_if, pad[1:])), bgt=b_if[:, :, None],
        conv_w=conv_w, conv_b=row(conv_b),
        wq=w_q.astype(BF16), wk=(w_k * (M_HEAD_DIM ** -0.5)).astype(BF16), wv=w_v.astype(BF16),
        hng=row(head_norm_g), skip=row(skip),
        pool_w=pool_w.astype(BF16), pool_scale=row(pool_scale), w_out=w_out.astype(BF16))


def _c_in_kernel(x_ref, g_ref, wcq_ref, wckv_ref, wkpe_ref, wz_ref, qng_ref, kvng_ref,
                 wuq_ref, wuk_ref, wuv_ref, qg_ref, kg_ref, tab_ref,
                 q_ref, k_ref, v_ref, sz_ref, *, sub):
    qg_n, qg_r = qg_ref[:, 0:QK_NOPE], qg_ref[:, QK_NOPE:QK_PAD]
    kg_n, kg_r = kg_ref[:, 0:QK_NOPE], kg_ref[:, QK_NOPE:QK_PAD]
    scale = QK_HEAD ** -0.5

    def rows_chain(r0):
        rows = slice(r0, r0 + sub)
        yb = _rms(x_ref[rows, :], g_ref[...]).astype(BF16)
        yield
        z = _dot(yb, wz_ref[...])
        cq = _dot(yb, wcq_ref[...])
        ckv = _dot(yb, wckv_ref[...])
        kpe = _dot(yb, wkpe_ref[...])
        yield
        cqn = _rms(cq, qng_ref[...]).astype(BF16)
        ckvn = _rms(ckv, kvng_ref[...]).astype(BF16)
        sz_ref[rows, :] = _silu(z).astype(BF16)
        tab = tab_ref[rows, :]
        q_tab = tab * qg_r
        kpe_ss = 0.5 * jnp.sum(kpe * kpe, axis=1, keepdims=True)
        kp = kpe * (tab * kg_r)
        kr = kp + pltpu.roll(kp, LANES // 2, axis=1)
        yield
        q = _dot(cqn, wuq_ref[...])
        kn = _dot(ckvn, wuk_ref[...])
        v = _dot(ckvn, wuv_ref[...]).astype(BF16)
        yield
        ones_col = (lax.broadcasted_iota(jnp.int32, (sub, V_PAD - V_HEAD), 1) == 0).astype(BF16)
        for h in range(C_HEADS):
            v_ref[rows, h * V_PAD:h * V_PAD + V_HEAD] = v[:, h * V_HEAD:(h + 1) * V_HEAD]
            v_ref[rows, h * V_PAD + V_HEAD:(h + 1) * V_PAD] = ones_col
            o = h * QK_PAD
            qn = q[:, o:o + QK_NOPE]
            qr = q[:, o + QK_NOPE:o + QK_PAD]
            ss = jnp.sum(qn * qn + 0.5 * (qr * qr), axis=1, keepdims=True)
            r = lax.rsqrt(ss * (1.0 / QK_HEAD) + RMS_EPS) * scale
            q_ref[rows, o:o + QK_NOPE] = (qn * r * qg_n).astype(BF16)
            q_ref[rows, o + QK_NOPE:o + QK_PAD] = (qr * r * q_tab).astype(BF16)
            knh = kn[:, h * QK_NOPE:(h + 1) * QK_NOPE]
            rk = lax.rsqrt((jnp.sum(knh * knh, axis=1, keepdims=True) + kpe_ss) * (1.0 / QK_HEAD) + RMS_EPS)
            k_ref[rows, o:o + QK_NOPE] = (knh * rk * kg_n).astype(BF16)
            k_ref[rows, o + QK_NOPE:o + QK_PAD] = (kr * rk).astype(BF16)
            if h % 2 == 1:
                yield

    _interleave([rows_chain(r0) for r0 in range(0, x_ref.shape[0], sub)], skew=2)


def _c_in(x2, p, j, rope_tab, tm, sub):
    n = x2.shape[0]
    tok = lambda w: pl.BlockSpec((tm, w), lambda i: (i, 0))
    names = ("norm_g", "wcq", "wckv", "wkpe", "wz", "qng", "kvng", "wuq", "wuk", "wuv", "qg", "kg")
    return pl.pallas_call(
        functools.partial(_c_in_kernel, sub=sub),
        grid=(n // tm,),
        in_specs=[tok(D_MODEL)] + [_layer_spec(p[k], j) for k in names] + [tok(LANES)],
        out_specs=[tok(C_HEADS * QK_PAD), tok(C_HEADS * QK_PAD), tok(C_HEADS * V_PAD), tok(D_MODEL)],
        out_shape=[jax.ShapeDtypeStruct((n, C_HEADS * QK_PAD), BF16),
                   jax.ShapeDtypeStruct((n, C_HEADS * QK_PAD), BF16),
                   jax.ShapeDtypeStruct((n, C_HEADS * V_PAD), BF16),
                   jax.ShapeDtypeStruct((n, D_MODEL), BF16)],
        compiler_params=_params(("parallel",)),
        name="c_in_proj",
    )(x2, *[p[k] for k in names], rope_tab)


def _attn_kernel(q_ref, k_ref, v_ref, sz_ref, o_ref, *, tq, heads):
    seq = q_ref.shape[0]
    nq = seq // tq
    rc = lax.broadcasted_iota(jnp.int32, (tq, tq), 0) // CHUNK
    cc = lax.broadcasted_iota(jnp.int32, (tq, tq), 1) // CHUNK
    mask = cc <= rc

    def row_max(blocks):
        m = blocks[0]
        for blk in blocks[1:]:
            m = jnp.maximum(m, blk)
        return jnp.max(m, axis=1, keepdims=True)

    def head_chain(h):
        qc = slice(h * QK_PAD, (h + 1) * QK_PAD)
        vc = slice(h * V_PAD, (h + 1) * V_PAD)

        def scores(qi, c):
            s = _dot_nt(q_ref[qi * tq:(qi + 1) * tq, qc], k_ref[c * tq:(c + 1) * tq, qc])
            return jnp.where(mask, s, -jnp.inf) if c == qi else s

        order = list(range(nq - 1, -1, -1))
        cur = [scores(order[0], c) for c in range(order[0] + 1)]
        yield
        for pos, qi in enumerate(order):
            m = row_max(cur)
            nq_i = order[pos + 1] if pos + 1 < nq else None
            nxt = []
            pb = []
            for c in range(qi + 1):
                pb.append(jnp.exp((cur[c] - m).astype(BF16)))
                if nq_i is not None and c <= nq_i:
                    nxt.append(scores(nq_i, c))
            p_all = pb[0] if len(pb) == 1 else jnp.concatenate(pb, axis=1)
            acc = _dot(p_all, v_ref[0:(qi + 1) * tq, vc])
            rows, oc = slice(qi * tq, (qi + 1) * tq), slice(h * V_HEAD, (h + 1) * V_HEAD)
            o = (acc[:, 0:V_HEAD] * (1.0 / acc[:, V_HEAD:V_HEAD + 1])).astype(BF16)
            o_ref[rows, oc] = (o.astype(F32) * sz_ref[rows, oc].astype(F32)).astype(BF16)
            cur = nxt
            yield

    _interleave([head_chain(h) for h in range(heads)])


def _attention(q, k, v, sz, batch, seq, tq, heads):
    n = q.shape[0]
    return pl.pallas_call(
        functools.partial(_attn_kernel, tq=tq, heads=heads),
        grid=(batch, C_HEADS // heads),
        in_specs=[pl.BlockSpec((seq, heads * QK_PAD), lambda b, h: (b, h)),
                  pl.BlockSpec((seq, heads * QK_PAD), lambda b, h: (b, h)),
                  pl.BlockSpec((seq, heads * V_PAD), lambda b, h: (b, h)),
                  pl.BlockSpec((seq, heads * V_HEAD), lambda b, h: (b, h))],
        out_specs=pl.BlockSpec((seq, heads * V_HEAD), lambda b, h: (b, h)),
        out_shape=jax.ShapeDtypeStruct((n, C_HEADS * V_HEAD), BF16),
        compiler_params=_params(("parallel", "parallel")),
        name="block_causal_attention",
    )(q, k, v, sz)


def _c_out_kernel(x_ref, o_ref, wo_ref, out_ref):
    out_ref[...] = x_ref[...] + _dot(o_ref[...], wo_ref[...])


def _c_out(x2, o, w_out, j, tm):
    n = x2.shape[0]
    tok = lambda w: pl.BlockSpec((tm, w), lambda i: (i, 0))
    return pl.pallas_call(
        _c_out_kernel,
        grid=(n // tm,),
        in_specs=[tok(D_MODEL), tok(D_MODEL), _layer_spec(w_out, j)],
        out_specs=tok(D_MODEL),
        out_shape=jax.ShapeDtypeStruct((n, D_MODEL), F32),
        compiler_params=_params(("parallel",)),
        name="c_out_proj",
    )(x2, o, w_out)


def _layer_c(x2, p, j, rope_tab, batch, seq):
    q, k, v, sz = _c_in(x2, p, j, rope_tab, tm=C_IN_TILE, sub=C_IN_SUB)
    o = _attention(q, k, v, sz, batch, seq, tq=ATTN_Q_BLOCK, heads=ATTN_HEADS_PER_STEP)
    return _c_out(x2, o, p["w_out"], j, tm=C_OUT_TILE)


def _pad_rope_cols(a):
    half = QK_ROPE // 2
    x1, x2 = a[..., :half], a[..., half:]
    return jnp.concatenate([x1, x2, x2, x1], axis=-1)


def _pad_head_cols(a):
    return jnp.concatenate([a[..., :QK_NOPE], _pad_rope_cols(a[..., QK_NOPE:])], axis=-1)


def _prep_c(norm_g, w_in, q_norm_g, kv_norm_g, w_uq, w_ukv, qn_g, kn_g, w_out):
    nl = w_in.shape[0]
    c0, c1, c2 = Q_LORA, Q_LORA + KV_LORA, Q_LORA + KV_LORA + QK_ROPE
    wuq = _pad_head_cols(w_uq.reshape(nl, Q_LORA, C_HEADS, QK_HEAD)).reshape(nl, Q_LORA, C_HEADS * QK_PAD)
    wukv = w_ukv.reshape(nl, KV_LORA, C_HEADS, QK_NOPE + V_HEAD)
    row = lambda a: a[:, None, :]
    return dict(
        norm_g=row(norm_g),
        wcq=w_in[..., :c0].astype(BF16), wckv=w_in[..., c0:c1].astype(BF16),
        wkpe=_pad_rope_cols(w_in[..., c1:c2]).astype(BF16), wz=w_in[..., c2:].astype(BF16),
        qng=row(q_norm_g), kvng=row(kv_norm_g),
        wuq=wuq.astype(BF16),
        wuk=wukv[..., :QK_NOPE].reshape(nl, KV_LORA, C_HEADS * QK_NOPE).astype(BF16),
        wuv=wukv[..., QK_NOPE:].reshape(nl, KV_LORA, C_HEADS * V_HEAD).astype(BF16),
        qg=row(_pad_head_cols(qn_g)), kg=row(_pad_head_cols(kn_g)),
        w_out=w_out.astype(BF16))


def kernel(x, positions, a_norm_g, a_w_in, a_b_if, a_conv_w, a_conv_b, a_w_q, a_w_k, a_w_v, a_head_norm_g, a_skip, a_pool_w, a_pool_scale, a_w_out, c_norm_g, c_w_in, c_q_norm_g, c_kv_norm_g, c_w_uq, c_w_ukv, c_qn_g, c_kn_g, c_w_out):
    batch, seq, d = x.shape
    depth = a_norm_g.shape[0] + c_norm_g.shape[0]
    a_args = (a_norm_g, a_w_in, a_b_if, a_conv_w, a_conv_b, a_w_q, a_w_k, a_w_v, a_head_norm_g, a_skip,
              a_pool_w, a_pool_scale, a_w_out)
    c_args = (c_norm_g, c_w_in, c_q_norm_g, c_kv_norm_g, c_w_uq, c_w_ukv, c_qn_g, c_kn_g, c_w_out)
    rope_tab = _rope_table(positions)
    pa, pc = _prep_a(*a_args), _prep_c(*c_args)
    x2 = x.reshape(batch * seq, d)
    for layer in range(depth):
        j = layer // 2
        if layer % 2 == 0:
            x2 = _layer_a(x2, pa, j, batch, seq)
        else:
            x2 = _layer_c(x2, pc, j, rope_tab, batch, seq)
    return x2.reshape(batch, seq, d)
```

```python
import functools

import jax
import jax.numpy as jnp
from jax import lax
from jax.experimental import pallas as pl
from jax.experimental.pallas import tpu as pltpu

F32 = jnp.float32
BF16 = jnp.bfloat16

D_MODEL = 1024
RMS_EPS = 1e-6
CHUNK = 64
M_HEADS = 4
M_HEAD_DIM = 256
CONV_K = 4
M_CHUNK = 256
P_WINDOWS = (2, 4, 8, 16)
P_GROUP_DIM = 256
HIST = 16
CONV_HIST = 8
C_HEADS = 8
QK_NOPE = 128
QK_ROPE = 64
QK_HEAD = QK_NOPE + QK_ROPE
QK_PAD = 256
V_HEAD = 128
V_PAD = 256
Q_LORA = 384
KV_LORA = 256
ROPE_THETA = 10000.0
LANES = 128

A_MIX_TILE = 2 * M_CHUNK
A_OUT_TILE, A_OUT_SUB = 1024, 256
C_IN_TILE, C_IN_SUB = 1024, 256
ATTN_Q_BLOCK, ATTN_HEADS_PER_STEP = 256, 2
C_OUT_TILE = 1024

V7X_VMEM_BYTES = 64 * 1024 * 1024
VMEM_LIMIT = V7X_VMEM_BYTES * 7 // 8


def _params(sem):
    return pltpu.CompilerParams(dimension_semantics=sem, vmem_limit_bytes=VMEM_LIMIT)


def _const_spec(shape):
    nd = len(shape)
    return pl.BlockSpec(shape, lambda *_: (0,) * nd, pipeline_mode=pl.Buffered(1))


def _layer_spec(stacked, j):
    nd = stacked.ndim - 1
    return pl.BlockSpec((None,) + stacked.shape[1:], lambda *_: (j,) + (0,) * nd, pipeline_mode=pl.Buffered(1))


def _rms(x, g):
    return x * lax.rsqrt(jnp.mean(x * x, axis=-1, keepdims=True) + RMS_EPS) * g


def _silu(x):
    h = 0.5 * x
    return h * jnp.tanh(h) + h


def _log_sigmoid(x):
    return jnp.minimum(x, 0.0) - jnp.log1p(jnp.exp(-jnp.abs(x)))


def _dot(a, b):
    return jnp.dot(a, b, preferred_element_type=F32)


def _dot_nt(a, b):
    return lax.dot_general(a, b, (((1,), (1,)), ((), ())), preferred_element_type=F32)


def _dot_tn(a, b):
    return lax.dot_general(a, b, (((0,), (0,)), ((), ())), preferred_element_type=F32)


def _interleave(chains, skew=0):
    live = dict(enumerate(chains))
    tick = 0
    while live:
        for i in sorted(live):
            if tick >= i * skew:
                try:
                    next(live[i])
                except StopIteration:
                    del live[i]
        tick += 1


def _rope_table_kernel(pos_ref, invf_ref, tab_ref):
    half = QK_ROPE // 2
    groups = LANES // half
    ang = pos_ref[...] * invf_ref[...]
    cos, sin = jnp.cos(ang), jnp.sin(ang)
    group = lax.broadcasted_iota(jnp.int32, ang.shape, 1) // half
    rolled = {0: (cos, sin)}
    for sh in range(half, LANES, half):
        rolled[sh] = (pltpu.roll(cos, sh, axis=1), pltpu.roll(sin, sh, axis=1))
    for k in range(groups):
        at = [rolled[((g - k) * half) % LANES] for g in range(groups)]
        tab_ref[k] = jnp.where(group == 0, at[0][0],
                               jnp.where(group == 1, at[1][0],
                                         jnp.where(group == 2, -at[2][1], at[3][1])))


def _rope_table(positions):
    n = positions.size
    half = QK_ROPE // 2
    per_row = LANES // half
    rows = n // per_row
    inv_freq = ROPE_THETA ** (-jnp.arange(0, QK_ROPE, 2, dtype=F32) / QK_ROPE)
    invf = jnp.tile(inv_freq, per_row)[None, :]
    pos = positions.reshape(per_row, rows).astype(F32).T
    pos = jnp.broadcast_to(pos[:, :, None], (rows, per_row, half)).reshape(rows, LANES)
    tr = min(rows, 2048)
    tab = pl.pallas_call(
        _rope_table_kernel,
        grid=(rows // tr,),
        in_specs=[pl.BlockSpec((tr, LANES), lambda i: (i, 0)), _const_spec((1, LANES))],
        out_specs=pl.BlockSpec((per_row, tr, LANES), lambda i: (0, i, 0)),
        out_shape=jax.ShapeDtypeStruct((per_row, rows, LANES), F32),
        compiler_params=_params(("parallel",)),
        name="rope_table",
    )(pos, invf)
    return tab.reshape(n, LANES)


def _a_mix_kernel(x_ref, g_ref, w3_ref, wg_ref, wgt_ref, bg_ref, bgt_ref, cw_ref, cb_ref,
                  wq_ref, wk_ref, wv_ref, hng_ref, skip_ref, ya_ref,
                  c_scr, n_scr, m_scr, hist_scr):
    t = M_CHUNK
    mw = M_HEADS * M_HEAD_DIM

    @pl.when(pl.program_id(1) == 0)
    def _():
        c_scr[...] = jnp.zeros_like(c_scr)
        n_scr[...] = jnp.zeros_like(n_scr)
        m_scr[...] = jnp.zeros_like(m_scr)
        hist_scr[...] = jnp.zeros_like(hist_scr)

    ri = lax.broadcasted_iota(jnp.int32, (t, t), 0)
    ci = lax.broadcasted_iota(jnp.int32, (t, t), 1)
    tril = ci <= ri
    triu = ri <= ci
    neg_inf = -jnp.inf

    def chunk_inputs(rows):
        yb = _rms(x_ref[rows, :], g_ref[...]).astype(BF16)
        graw = _dot(yb, wg_ref[...]) + bg_ref[...]
        lane = lax.broadcasted_iota(jnp.int32, graw.shape, 1)
        gcol = jnp.where(lane >= M_HEADS, _log_sigmoid(graw), graw)
        grawt = _dot_nt(wgt_ref[...], yb) + bgt_ref[...]
        row = lax.broadcasted_iota(jnp.int32, grawt.shape, 0)
        grow = jnp.where(row >= M_HEADS, _log_sigmoid(grawt), grawt)
        return yb, gcol, grow

    row_slices = [slice(r0, r0 + t) for r0 in range(0, x_ref.shape[0], t)]
    inputs = [chunk_inputs(rows) for rows in row_slices]

    def head_chain(h):
        hs = slice(h * M_HEAD_DIM, (h + 1) * M_HEAD_DIM)
        for rows, (yb, gcol, grow) in zip(row_slices, inputs):
            i_row = grow[h:h + 1, :]
            lf_row = grow[M_HEADS + h:M_HEADS + h + 1, :]
            i_col = gcol[:, h:h + 1]
            lf_col = gcol[:, M_HEADS + h:M_HEADS + h + 1]
            b_col = jnp.sum(jnp.where(tril, lf_row, 0.0), axis=1, keepdims=True)
            b_row = jnp.sum(jnp.where(triu, lf_col, 0.0), axis=0, keepdims=True)
            a_row = i_row - b_row
            a_col = i_col - b_col
            m_old = m_scr[h][:, 0:1]
            big_m = jnp.maximum(jnp.max(jnp.where(tril, a_row, neg_inf), axis=1, keepdims=True), m_old)
            decay = jnp.exp(jnp.where(tril, a_row - big_m, neg_inf))
            w_inter = jnp.exp(m_old - big_m)
            m_last = jnp.maximum(jnp.max(a_row, axis=1, keepdims=True), m_old)
            b_last = jnp.sum(lf_row, axis=1, keepdims=True)
            w_state = jnp.exp(a_col - m_last)
            wc = jnp.exp(m_old - m_last)
            inv_floor = jnp.exp(-(b_col + big_m))
            m_scr[h] = jnp.broadcast_to(b_last + m_last, (1, LANES))
            yield
            xm = _dot(yb, w3_ref[:, hs])
            yield
            xe = jnp.concatenate([hist_scr[:, hs], xm], axis=0)
            hist_scr[:, hs] = xm[t - CONV_HIST:, :]
            x1 = pltpu.roll(xe, 1, axis=0)
            u = cw_ref[3:4, hs] * xe + cw_ref[2:3, hs] * x1
            v2 = pltpu.roll(cw_ref[1:2, hs] * xe + cw_ref[0:1, hs] * x1, 2, axis=0)
            conv = cb_ref[:, hs] + (u + v2)[CONV_HIST:, :]
            xc = _silu(conv)
            xcb = xc.astype(BF16)
            yield
            q = _dot(xcb, wq_ref[h])
            k = _dot(xcb, wk_ref[h])
            v = _dot(xm.astype(BF16), wv_ref[h])
            qb, vb = q.astype(BF16), v.astype(BF16)
            yield
            s = _dot_nt(qb, k.astype(BF16)) * decay
            n_row = n_scr[h]
            den = (jnp.sum(s, axis=1, keepdims=True)
                   + w_inter * jnp.sum(q * n_row, axis=1, keepdims=True))
            yield
            num = _dot(s.astype(BF16), vb) + w_inter * _dot(qb, c_scr[h].astype(BF16))
            hh_half = num * (0.5 / jnp.maximum(jnp.abs(den), inv_floor))
            kw = k * w_state
            c_scr[h] = wc * c_scr[h] + _dot_tn(kw.astype(BF16), vb)
            n_scr[h] = wc * n_row + jnp.sum(kw, axis=0, keepdims=True)
            yield
            om = _dot(yb, w3_ref[:, mw + h * M_HEAD_DIM:mw + (h + 1) * M_HEAD_DIM])
            zm = _dot(yb, w3_ref[:, 2 * mw + h * M_HEAD_DIM:2 * mw + (h + 1) * M_HEAD_DIM])
            yield
            hm = _rms(hh_half * jnp.tanh(0.5 * om) + hh_half, hng_ref[:, hs])
            ya_ref[rows, hs] = ((hm + skip_ref[:, hs] * xc) * _silu(zm)).astype(BF16)

    _interleave([head_chain(h) for h in range(M_HEADS)])


def _a_mix(x2, p, j, batch, seq):
    n = x2.shape[0]
    t = A_MIX_TILE
    spt = seq // t
    tok = lambda w: pl.BlockSpec((t, w), lambda b, s: (b * spt + s, 0))
    names = ("norm_g", "w3", "wg", "wgt", "bg", "bgt", "conv_w", "conv_b", "wq", "wk", "wv", "hng", "skip")
    return pl.pallas_call(
        _a_mix_kernel,
        grid=(batch, spt),
        in_specs=[tok(D_MODEL)] + [_layer_spec(p[k], j) for k in names],
        out_specs=tok(D_MODEL),
        out_shape=jax.ShapeDtypeStruct((n, D_MODEL), BF16),
        scratch_shapes=[pltpu.VMEM((M_HEADS, M_HEAD_DIM, M_HEAD_DIM), F32),
                        pltpu.VMEM((M_HEADS, 1, M_HEAD_DIM), F32),
                        pltpu.VMEM((M_HEADS, 1, LANES), F32),
                        pltpu.VMEM((CONV_HIST, D_MODEL), F32)],
        compiler_params=_params(("parallel", "arbitrary")),
        name="a_mlstm",
    )(x2, *[p[k] for k in names])


def _a_out_kernel(x_ref, ya_ref, g_ref, wp_ref, pw_ref, ps_ref, wo_ref, o_ref, hist_scr, *, sub):
    tm = x_ref.shape[0]
    si = pl.program_id(1)

    @pl.when(si == 0)
    def _():
        hist_scr[...] = jnp.zeros_like(hist_scr)

    pw_all = len(P_WINDOWS) * P_GROUP_DIM
    row_slices = [slice(r0, r0 + sub) for r0 in range(0, tm, sub)]
    parts = [[None] * (len(P_WINDOWS) + 1) for _ in row_slices]
    xs = [x_ref[rows, :] for rows in row_slices]
    ybs = [_rms(x, g_ref[...]).astype(BF16) for x in xs]

    def skip_chain():
        for i, rows in enumerate(row_slices):
            parts[i][0] = xs[i] + _dot(ya_ref[rows, :], wo_ref[0:D_MODEL, :])
            yield

    def group_chain(g, w):
        gs = slice(g * P_GROUP_DIM, (g + 1) * P_GROUP_DIM)
        for i, rows in enumerate(row_slices):
            yb = ybs[i]
            tpos = si * tm + rows.start + lax.broadcasted_iota(jnp.int32, (sub, 1), 0)
            xp = _dot(yb, wp_ref[:, gs])
            zp = _dot(yb, wp_ref[:, pw_all + g * P_GROUP_DIM:pw_all + (g + 1) * P_GROUP_DIM])
            yield
            xe = jnp.concatenate([hist_scr[:, gs], xp], axis=0)
            hist_scr[:, gs] = xp[sub - HIST:, :]
            tot = xe
            sh = 1
            while sh < w:
                tot = tot + pltpu.roll(tot, sh, axis=0)
                sh *= 2
            inv_cnt = 1.0 / jnp.minimum(tpos + 1, w).astype(F32)
            mix = (tot[HIST:, :] * inv_cnt - xp).astype(BF16)
            gate = ps_ref[:, gs] * _silu(zp)
            yield
            yb_g = (_dot(mix, pw_ref[g]) * gate).astype(BF16)
            yield
            parts[i][g + 1] = _dot(yb_g, wo_ref[D_MODEL + g * P_GROUP_DIM:D_MODEL + (g + 1) * P_GROUP_DIM, :])
            yield

    _interleave([skip_chain()] + [group_chain(g, w) for g, w in enumerate(P_WINDOWS)])
    for i, rows in enumerate(row_slices):
        acc = parts[i][0]
        for part in parts[i][1:]:
            acc = acc + part
        o_ref[rows, :] = acc


def _a_out(x2, ya, p, j, batch, seq, tm, sub):
    n = x2.shape[0]
    spt = seq // tm
    tok = lambda w: pl.BlockSpec((tm, w), lambda b, s: (b * spt + s, 0))
    names = ("norm_g", "wp", "pool_w", "pool_scale", "w_out")
    return pl.pallas_call(
        functools.partial(_a_out_kernel, sub=sub),
        grid=(batch, spt),
        in_specs=[tok(D_MODEL), tok(D_MODEL)] + [_layer_spec(p[k], j) for k in names],
        out_specs=tok(D_MODEL),
        out_shape=jax.ShapeDtypeStruct((n, D_MODEL), F32),
        scratch_shapes=[pltpu.VMEM((HIST, D_MODEL), F32)],
        compiler_params=_params(("parallel", "arbitrary")),
        name="a_pool_out_proj",
    )(x2, ya, *[p[k] for k in names])


def _layer_a(x2, p, j, batch, seq):
    ya = _a_mix(x2, p, j, batch, seq)
    return _a_out(x2, ya, p, j, batch, seq, tm=A_OUT_TILE, sub=A_OUT_SUB)


def _prep_a(norm_g, w_in, b_if, conv_w, conv_b, w_q, w_k, w_v, head_norm_g, skip, pool_w, pool_scale, w_out):
    mw = M_HEADS * M_HEAD_DIM
    g0 = 2 * mw
    g1 = g0 + 2 * M_HEADS
    z1 = g1 + mw
    wg = w_in[..., g0:g1]
    pad = [(0, 0)] * (wg.ndim - 1) + [(0, LANES - 2 * M_HEADS)]
    row = lambda a: a[:, None, :]
    return dict(
        norm_g=row(norm_g),
        w3=jnp.concatenate([w_in[..., :g0], w_in[..., g1:z1]], axis=-1).astype(BF16),
        wp=w_in[..., z1:].astype(BF16),
        wg=jnp.pad(wg, pad).astype(BF16), wgt=jnp.swapaxes(wg, -1, -2).astype(BF16),
        bg=row(jnp.pad(b_if, pad[1:])), bgt=b_if[:, :, None],
        conv_w=conv_w, conv_b=row(conv_b),
        wq=w_q.astype(BF16), wk=(w_k * (M_HEAD_DIM ** -0.5)).astype(BF16), wv=w_v.astype(BF16),
        hng=row(head_norm_g), skip=row(skip),
        pool_w=pool_w.astype(BF16), pool_scale=row(pool_scale), w_out=w_out.astype(BF16))


def _c_in_kernel(x_ref, g_ref, wcq_ref, wckv_ref, wkpe_ref, wz_ref, qng_ref, kvng_ref,
                 wuq_ref, wuk_ref, wuv_ref, qg_ref, kg_ref, tab_ref,
                 q_ref, k_ref, v_ref, sz_ref, *, sub):
    qg_n, qg_r = qg_ref[:, 0:QK_NOPE], qg_ref[:, QK_NOPE:QK_PAD]
    kg_n, kg_r = kg_ref[:, 0:QK_NOPE], kg_ref[:, QK_NOPE:QK_PAD]
    scale = QK_HEAD ** -0.5

    def rows_chain(r0):
        rows = slice(r0, r0 + sub)
        yb = _rms(x_ref[rows, :], g_ref[...]).astype(BF16)
        yield
        z = _dot(yb, wz_ref[...])
        cq = _dot(yb, wcq_ref[...])
        ckv = _dot(yb, wckv_ref[...])
        kpe = _dot(yb, wkpe_ref[...])
        yield
        cqn = _rms(cq, qng_ref[...]).astype(BF16)
        ckvn = _rms(ckv, kvng_ref[...]).astype(BF16)
        sz_ref[rows, :] = _silu(z).astype(BF16)
        tab = tab_ref[rows, :]
        q_tab = tab * qg_r
        kpe_ss = 0.5 * jnp.sum(kpe * kpe, axis=1, keepdims=True)
        kp = kpe * (tab * kg_r)
        kr = kp + pltpu.roll(kp, LANES // 2, axis=1)
        yield
        q = _dot(cqn, wuq_ref[...])
        kn = _dot(ckvn, wuk_ref[...])
        v = _dot(ckvn, wuv_ref[...]).astype(BF16)
        yield
        ones_col = (lax.broadcasted_iota(jnp.int32, (sub, V_PAD - V_HEAD), 1) == 0).astype(BF16)
        for h in range(C_HEADS):
            v_ref[rows, h * V_PAD:h * V_PAD + V_HEAD] = v[:, h * V_HEAD:(h + 1) * V_HEAD]
            v_ref[rows, h * V_PAD + V_HEAD:(h + 1) * V_PAD] = ones_col
            o = h * QK_PAD
            qn = q[:, o:o + QK_NOPE]
            qr = q[:, o + QK_NOPE:o + QK_PAD]
            ss = jnp.sum(qn * qn + 0.5 * (qr * qr), axis=1, keepdims=True)
            r = lax.rsqrt(ss * (1.0 / QK_HEAD) + RMS_EPS) * scale
            q_ref[rows, o:o + QK_NOPE] = (qn * r * qg_n).astype(BF16)
            q_ref[rows, o + QK_NOPE:o + QK_PAD] = (qr * r * q_tab).astype(BF16)
            knh = kn[:, h * QK_NOPE:(h + 1) * QK_NOPE]
            rk = lax.rsqrt((jnp.sum(knh * knh, axis=1, keepdims=True) + kpe_ss) * (1.0 / QK_HEAD) + RMS_EPS)
            k_ref[rows, o:o + QK_NOPE] = (knh * rk * kg_n).astype(BF16)
            k_ref[rows, o + QK_NOPE:o + QK_PAD] = (kr * rk).astype(BF16)
            if h % 2 == 1:
                yield

    _interleave([rows_chain(r0) for r0 in range(0, x_ref.shape[0], sub)], skew=2)


def _c_in(x2, p, j, rope_tab, tm, sub):
    n = x2.shape[0]
    tok = lambda w: pl.BlockSpec((tm, w), lambda i: (i, 0))
    names = ("norm_g", "wcq", "wckv", "wkpe", "wz", "qng", "kvng", "wuq", "wuk", "wuv", "qg", "kg")
    return pl.pallas_call(
        functools.partial(_c_in_kernel, sub=sub),
        grid=(n // tm,),
        in_specs=[tok(D_MODEL)] + [_layer_spec(p[k], j) for k in names] + [tok(LANES)],
        out_specs=[tok(C_HEADS * QK_PAD), tok(C_HEADS * QK_PAD), tok(C_HEADS * V_PAD), tok(D_MODEL)],
        out_shape=[jax.ShapeDtypeStruct((n, C_HEADS * QK_PAD), BF16),
                   jax.ShapeDtypeStruct((n, C_HEADS * QK_PAD), BF16),
                   jax.ShapeDtypeStruct((n, C_HEADS * V_PAD), BF16),
                   jax.ShapeDtypeStruct((n, D_MODEL), BF16)],
        compiler_params=_params(("parallel",)),
        name="c_in_proj",
    )(x2, *[p[k] for k in names], rope_tab)


def _attn_kernel(q_ref, k_ref, v_ref, sz_ref, o_ref, *, tq, heads):
    seq = q_ref.shape[0]
    nq = seq // tq
    rc = lax.broadcasted_iota(jnp.int32, (tq, tq), 0) // CHUNK
    cc = lax.broadcasted_iota(jnp.int32, (tq, tq), 1) // CHUNK
    mask = cc <= rc

    def row_max(blocks):
        m = blocks[0]
        for blk in blocks[1:]:
            m = jnp.maximum(m, blk)
        return jnp.max(m, axis=1, keepdims=True)

    def head_chain(h):
        qc = slice(h * QK_PAD, (h + 1) * QK_PAD)
        vc = slice(h * V_PAD, (h + 1) * V_PAD)

        def scores(qi, c):
            s = _dot_nt(q_ref[qi * tq:(qi + 1) * tq, qc], k_ref[c * tq:(c + 1) * tq, qc])
            return jnp.where(mask, s, -jnp.inf) if c == qi else s

        order = list(range(nq - 1, -1, -1))
        cur = [scores(order[0], c) for c in range(order[0] + 1)]
        yield
        for pos, qi in enumerate(order):
            m = row_max(cur)
            nq_i = order[pos + 1] if pos + 1 < nq else None
            nxt = []
            pb = []
            for c in range(qi + 1):
                pb.append(jnp.exp((cur[c] - m).astype(BF16)))
                if nq_i is not None and c <= nq_i:
                    nxt.append(scores(nq_i, c))
            p_all = pb[0] if len(pb) == 1 else jnp.concatenate(pb, axis=1)
            acc = _dot(p_all, v_ref[0:(qi + 1) * tq, vc])
            rows, oc = slice(qi * tq, (qi + 1) * tq), slice(h * V_HEAD, (h + 1) * V_HEAD)
            o = acc[:, 0:V_HEAD] * (1.0 / acc[:, V_HEAD:V_HEAD + 1])
            o_ref[rows, oc] = (o * sz_ref[rows, oc].astype(F32)).astype(BF16)
            cur = nxt
            yield

    _interleave([head_chain(h) for h in range(heads)])


def _attention(q, k, v, sz, batch, seq, tq, heads):
    n = q.shape[0]
    return pl.pallas_call(
        functools.partial(_attn_kernel, tq=tq, heads=heads),
        grid=(batch, C_HEADS // heads),
        in_specs=[pl.BlockSpec((seq, heads * QK_PAD), lambda b, h: (b, h)),
                  pl.BlockSpec((seq, heads * QK_PAD), lambda b, h: (b, h)),
                  pl.BlockSpec((seq, heads * V_PAD), lambda b, h: (b, h)),
                  pl.BlockSpec((seq, heads * V_HEAD), lambda b, h: (b, h))],
        out_specs=pl.BlockSpec((seq, heads * V_HEAD), lambda b, h: (b, h)),
        out_shape=jax.ShapeDtypeStruct((n, C_HEADS * V_HEAD), BF16),
        compiler_params=_params(("parallel", "parallel")),
        name="block_causal_attention",
    )(q, k, v, sz)


def _c_out_kernel(x_ref, o_ref, wo_ref, out_ref):
    out_ref[...] = x_ref[...] + _dot(o_ref[...], wo_ref[...])


def _c_out(x2, o, w_out, j, tm):
    n = x2.shape[0]
    tok = lambda w: pl.BlockSpec((tm, w), lambda i: (i, 0))
    return pl.pallas_call(
        _c_out_kernel,
        grid=(n // tm,),
        in_specs=[tok(D_MODEL), tok(D_MODEL), _layer_spec(w_out, j)],
        out_specs=tok(D_MODEL),
        out_shape=jax.ShapeDtypeStruct((n, D_MODEL), F32),
        compiler_params=_params(("parallel",)),
        name="c_out_proj",
    )(x2, o, w_out)


def _layer_c(x2, p, j, rope_tab, batch, seq):
    q, k, v, sz = _c_in(x2, p, j, rope_tab, tm=C_IN_TILE, sub=C_IN_SUB)
    o = _attention(q, k, v, sz, batch, seq, tq=ATTN_Q_BLOCK, heads=ATTN_HEADS_PER_STEP)
    return _c_out(x2, o, p["w_out"], j, tm=C_OUT_TILE)


def _pad_rope_cols(a):
    half = QK_ROPE // 2
    x1, x2 = a[..., :half], a[..., half:]
    return jnp.concatenate([x1, x2, x2, x1], axis=-1)


def _pad_head_cols(a):
    return jnp.concatenate([a[..., :QK_NOPE], _pad_rope_cols(a[..., QK_NOPE:])], axis=-1)


def _prep_c(norm_g, w_in, q_norm_g, kv_norm_g, w_uq, w_ukv, qn_g, kn_g, w_out):
    nl = w_in.shape[0]
    c0, c1, c2 = Q_LORA, Q_LORA + KV_LORA, Q_LORA + KV_LORA + QK_ROPE
    wuq = _pad_head_cols(w_uq.reshape(nl, Q_LORA, C_HEADS, QK_HEAD)).reshape(nl, Q_LORA, C_HEADS * QK_PAD)
    wukv = w_ukv.reshape(nl, KV_LORA, C_HEADS, QK_NOPE + V_HEAD)
    row = lambda a: a[:, None, :]
    return dict(
        norm_g=row(norm_g),
        wcq=w_in[..., :c0].astype(BF16), wckv=w_in[..., c0:c1].astype(BF16),
        wkpe=_pad_rope_cols(w_in[..., c1:c2]).astype(BF16), wz=w_in[..., c2:].astype(BF16),
        qng=row(q_norm_g), kvng=row(kv_norm_g),
        wuq=wuq.astype(BF16),
        wuk=wukv[..., :QK_NOPE].reshape(nl, KV_LORA, C_HEADS * QK_NOPE).astype(BF16),
        wuv=wukv[..., QK_NOPE:].reshape(nl, KV_LORA, C_HEADS * V_HEAD).astype(BF16),
        qg=row(_pad_head_cols(qn_g)), kg=row(_pad_head_cols(kn_g)),
        w_out=w_out.astype(BF16))


def kernel(x, positions, a_norm_g, a_w_in, a_b_if, a_conv_w, a_conv_b, a_w_q, a_w_k, a_w_v, a_head_norm_g, a_skip, a_pool_w, a_pool_scale, a_w_out, c_norm_g, c_w_in, c_q_norm_g, c_kv_norm_g, c_w_uq, c_w_ukv, c_qn_g, c_kn_g, c_w_out):
    batch, seq, d = x.shape
    depth = a_norm_g.shape[0] + c_norm_g.shape[0]
    a_args = (a_norm_g, a_w_in, a_b_if, a_conv_w, a_conv_b, a_w_q, a_w_k, a_w_v, a_head_norm_g, a_skip,
              a_pool_w, a_pool_scale, a_w_out)
    c_args = (c_norm_g, c_w_in, c_q_norm_g, c_kv_norm_g, c_w_uq, c_w_ukv, c_qn_g, c_kn_g, c_w_out)
    rope_tab = _rope_table(positions)
    pa, pc = _prep_a(*a_args), _prep_c(*c_args)
    x2 = x.reshape(batch * seq, d)
    for layer in range(depth):
        j = layer // 2
        if layer % 2 == 0:
            x2 = _layer_a(x2, pa, j, batch, seq)
        else:
            x2 = _layer_c(x2, pc, j, rope_tab, batch, seq)
    return x2.reshape(batch, seq, d)
```

```python
import functools

import jax
import jax.numpy as jnp
from jax import lax
from jax.experimental import pallas as pl
from jax.experimental.pallas import tpu as pltpu

F32 = jnp.float32
BF16 = jnp.bfloat16

D_MODEL = 1024
RMS_EPS = 1e-6
CHUNK = 64
M_HEADS = 4
M_HEAD_DIM = 256
CONV_K = 4
M_CHUNK = 256
P_WINDOWS = (2, 4, 8, 16)
P_GROUP_DIM = 256
HIST = 16
CONV_HIST = 8
C_HEADS = 8
QK_NOPE = 128
QK_ROPE = 64
QK_HEAD = QK_NOPE + QK_ROPE
QK_PAD = 256
V_HEAD = 128
V_PAD = 256
Q_LORA = 384
KV_LORA = 256
ROPE_THETA = 10000.0
LANES = 128

A_MIX_TILE = 2 * M_CHUNK
A_OUT_TILE, A_OUT_SUB = 1024, 256
C_IN_TILE, C_IN_SUB = 1024, 256
ATTN_Q_BLOCK, ATTN_HEADS_PER_STEP = 256, 2
C_OUT_TILE = 1024

V7X_VMEM_BYTES = 64 * 1024 * 1024
VMEM_LIMIT = V7X_VMEM_BYTES * 7 // 8


def _params(sem):
    return pltpu.CompilerParams(dimension_semantics=sem, vmem_limit_bytes=VMEM_LIMIT)


def _const_spec(shape):
    nd = len(shape)
    return pl.BlockSpec(shape, lambda *_: (0,) * nd, pipeline_mode=pl.Buffered(1))


def _layer_spec(stacked, j):
    nd = stacked.ndim - 1
    return pl.BlockSpec((None,) + stacked.shape[1:], lambda *_: (j,) + (0,) * nd, pipeline_mode=pl.Buffered(1))


def _rms(x, g):
    return x * lax.rsqrt(jnp.mean(x * x, axis=-1, keepdims=True) + RMS_EPS) * g


def _silu(x):
    h = 0.5 * x
    return h * jnp.tanh(h) + h


def _log_sigmoid(x):
    return jnp.minimum(x, 0.0) - jnp.log1p(jnp.exp(-jnp.abs(x)))


def _dot(a, b):
    return jnp.dot(a, b, preferred_element_type=F32)


def _dot_nt(a, b):
    return lax.dot_general(a, b, (((1,), (1,)), ((), ())), preferred_element_type=F32)


def _dot_tn(a, b):
    return lax.dot_general(a, b, (((0,), (0,)), ((), ())), preferred_element_type=F32)


def _interleave(chains, skew=0):
    live = dict(enumerate(chains))
    tick = 0
    while live:
        for i in sorted(live):
            if tick >= i * skew:
                try:
                    next(live[i])
                except StopIteration:
                    del live[i]
        tick += 1


def _rope_table_kernel(pos_ref, invf_ref, tab_ref):
    half = QK_ROPE // 2
    groups = LANES // half
    ang = pos_ref[...] * invf_ref[...]
    cos, sin = jnp.cos(ang), jnp.sin(ang)
    group = lax.broadcasted_iota(jnp.int32, ang.shape, 1) // half
    rolled = {0: (cos, sin)}
    for sh in range(half, LANES, half):
        rolled[sh] = (pltpu.roll(cos, sh, axis=1), pltpu.roll(sin, sh, axis=1))
    for k in range(groups):
        at = [rolled[((g - k) * half) % LANES] for g in range(groups)]
        tab_ref[k] = jnp.where(group == 0, at[0][0],
                               jnp.where(group == 1, at[1][0],
                                         jnp.where(group == 2, -at[2][1], at[3][1])))


def _rope_table(positions):
    n = positions.size
    half = QK_ROPE // 2
    per_row = LANES // half
    rows = n // per_row
    inv_freq = ROPE_THETA ** (-jnp.arange(0, QK_ROPE, 2, dtype=F32) / QK_ROPE)
    invf = jnp.tile(inv_freq, per_row)[None, :]
    pos = positions.reshape(per_row, rows).astype(F32).T
    pos = jnp.broadcast_to(pos[:, :, None], (rows, per_row, half)).reshape(rows, LANES)
    tr = min(rows, 2048)
    assert n % per_row == 0 and rows % tr == 0, (n, tr)
    tab = pl.pallas_call(
        _rope_table_kernel,
        grid=(rows // tr,),
        in_specs=[pl.BlockSpec((tr, LANES), lambda i: (i, 0)), _const_spec((1, LANES))],
        out_specs=pl.BlockSpec((per_row, tr, LANES), lambda i: (0, i, 0)),
        out_shape=jax.ShapeDtypeStruct((per_row, rows, LANES), F32),
        compiler_params=_params(("parallel",)),
        name="rope_table",
    )(pos, invf)
    return tab.reshape(n, LANES)


def _a_mix_kernel(x_ref, g_ref, w3_ref, wg_ref, wgt_ref, bg_ref, bgt_ref, cw_ref, cb_ref,
                  wq_ref, wk_ref, wv_ref, hng_ref, skip_ref, ya_ref,
                  c_scr, n_scr, m_scr, hist_scr):
    t = M_CHUNK
    mw = M_HEADS * M_HEAD_DIM

    @pl.when(pl.program_id(1) == 0)
    def _():
        c_scr[...] = jnp.zeros_like(c_scr)
        n_scr[...] = jnp.zeros_like(n_scr)
        m_scr[...] = jnp.zeros_like(m_scr)
        hist_scr[...] = jnp.zeros_like(hist_scr)

    ri = lax.broadcasted_iota(jnp.int32, (t, t), 0)
    ci = lax.broadcasted_iota(jnp.int32, (t, t), 1)
    tril = ci <= ri
    triu = ri <= ci
    neg_inf = -jnp.inf

    def chunk_inputs(rows):
        yb = _rms(x_ref[rows, :], g_ref[...]).astype(BF16)
        graw = _dot(yb, wg_ref[...]) + bg_ref[...]
        lane = lax.broadcasted_iota(jnp.int32, graw.shape, 1)
        gcol = jnp.where(lane >= M_HEADS, _log_sigmoid(graw), graw)
        grawt = _dot_nt(wgt_ref[...], yb) + bgt_ref[...]
        row = lax.broadcasted_iota(jnp.int32, grawt.shape, 0)
        grow = jnp.where(row >= M_HEADS, _log_sigmoid(grawt), grawt)
        return yb, gcol, grow

    row_slices = [slice(r0, r0 + t) for r0 in range(0, x_ref.shape[0], t)]
    inputs = [chunk_inputs(rows) for rows in row_slices]

    def head_chain(h):
        hs = slice(h * M_HEAD_DIM, (h + 1) * M_HEAD_DIM)
        for rows, (yb, gcol, grow) in zip(row_slices, inputs):
            i_row = grow[h:h + 1, :]
            lf_row = grow[M_HEADS + h:M_HEADS + h + 1, :]
            i_col = gcol[:, h:h + 1]
            lf_col = gcol[:, M_HEADS + h:M_HEADS + h + 1]
            b_col = jnp.sum(jnp.where(tril, lf_row, 0.0), axis=1, keepdims=True)
            b_row = jnp.sum(jnp.where(triu, lf_col, 0.0), axis=0, keepdims=True)
            a_row = i_row - b_row
            a_col = i_col - b_col
            m_old = m_scr[h][:, 0:1]
            big_m = jnp.maximum(jnp.max(jnp.where(tril, a_row, neg_inf), axis=1, keepdims=True), m_old)
            decay = jnp.exp(jnp.where(tril, a_row - big_m, neg_inf))
            w_inter = jnp.exp(m_old - big_m)
            m_last = jnp.maximum(jnp.max(a_row, axis=1, keepdims=True), m_old)
            b_last = jnp.sum(lf_row, axis=1, keepdims=True)
            w_state = jnp.exp(a_col - m_last)
            wc = jnp.exp(m_old - m_last)
            inv_floor = jnp.exp(-(b_col + big_m))
            m_scr[h] = jnp.broadcast_to(b_last + m_last, (1, LANES))
            yield
            xm = _dot(yb, w3_ref[:, hs])
            yield
            xe = jnp.concatenate([hist_scr[:, hs], xm], axis=0)
            hist_scr[:, hs] = xm[t - CONV_HIST:, :]
            x1 = pltpu.roll(xe, 1, axis=0)
            u = cw_ref[3:4, hs] * xe + cw_ref[2:3, hs] * x1
            v2 = pltpu.roll(cw_ref[1:2, hs] * xe + cw_ref[0:1, hs] * x1, 2, axis=0)
            conv = cb_ref[:, hs] + (u + v2)[CONV_HIST:, :]
            xc = _silu(conv)
            xcb = xc.astype(BF16)
            yield
            q = _dot(xcb, wq_ref[h])
            k = _dot(xcb, wk_ref[h])
            v = _dot(xm.astype(BF16), wv_ref[h])
            qb, vb = q.astype(BF16), v.astype(BF16)
            yield
            s = _dot_nt(qb, k.astype(BF16)) * decay
            n_row = n_scr[h]
            den = (jnp.sum(s, axis=1, keepdims=True)
                   + w_inter * jnp.sum(q * n_row, axis=1, keepdims=True))
            yield
            num = _dot(s.astype(BF16), vb) + w_inter * _dot(qb, c_scr[h].astype(BF16))
            hh_half = num * (0.5 / jnp.maximum(jnp.abs(den), inv_floor))
            kw = k * w_state
            c_scr[h] = wc * c_scr[h] + _dot_tn(kw.astype(BF16), vb)
            n_scr[h] = wc * n_row + jnp.sum(kw, axis=0, keepdims=True)
            yield
            om = _dot(yb, w3_ref[:, mw + h * M_HEAD_DIM:mw + (h + 1) * M_HEAD_DIM])
            zm = _dot(yb, w3_ref[:, 2 * mw + h * M_HEAD_DIM:2 * mw + (h + 1) * M_HEAD_DIM])
            yield
            hm = _rms(hh_half * jnp.tanh(0.5 * om) + hh_half, hng_ref[:, hs])
            ya_ref[rows, hs] = ((hm + skip_ref[:, hs] * xc) * _silu(zm)).astype(BF16)

    _interleave([head_chain(h) for h in range(M_HEADS)])


def _a_mix(x2, p, j, batch, seq):
    n = x2.shape[0]
    t = A_MIX_TILE
    spt = seq // t
    tok = lambda w: pl.BlockSpec((t, w), lambda b, s: (b * spt + s, 0))
    names = ("norm_g", "w3", "wg", "wgt", "bg", "bgt", "conv_w", "conv_b", "wq", "wk", "wv", "hng", "skip")
    return pl.pallas_call(
        _a_mix_kernel,
        grid=(batch, spt),
        in_specs=[tok(D_MODEL)] + [_layer_spec(p[k], j) for k in names],
        out_specs=tok(D_MODEL),
        out_shape=jax.ShapeDtypeStruct((n, D_MODEL), BF16),
        scratch_shapes=[pltpu.VMEM((M_HEADS, M_HEAD_DIM, M_HEAD_DIM), F32),
                        pltpu.VMEM((M_HEADS, 1, M_HEAD_DIM), F32),
                        pltpu.VMEM((M_HEADS, 1, LANES), F32),
                        pltpu.VMEM((CONV_HIST, D_MODEL), F32)],
        compiler_params=_params(("parallel", "arbitrary")),
        name="a_mlstm",
    )(x2, *[p[k] for k in names])


def _a_out_kernel(x_ref, ya_ref, g_ref, wp_ref, pw_ref, ps_ref, wo_ref, o_ref, hist_scr, *, sub):
    tm = x_ref.shape[0]
    si = pl.program_id(1)

    @pl.when(si == 0)
    def _():
        hist_scr[...] = jnp.zeros_like(hist_scr)

    pw_all = len(P_WINDOWS) * P_GROUP_DIM
    row_slices = [slice(r0, r0 + sub) for r0 in range(0, tm, sub)]
    parts = [[None] * (len(P_WINDOWS) + 1) for _ in row_slices]
    xs = [x_ref[rows, :] for rows in row_slices]
    ybs = [_rms(x, g_ref[...]).astype(BF16) for x in xs]

    def skip_chain():
        for i, rows in enumerate(row_slices):
            parts[i][0] = xs[i] + _dot(ya_ref[rows, :], wo_ref[0:D_MODEL, :])
            yield

    def group_chain(g, w):
        gs = slice(g * P_GROUP_DIM, (g + 1) * P_GROUP_DIM)
        for i, rows in enumerate(row_slices):
            yb = ybs[i]
            tpos = si * tm + rows.start + lax.broadcasted_iota(jnp.int32, (sub, 1), 0)
            xp = _dot(yb, wp_ref[:, gs])
            zp = _dot(yb, wp_ref[:, pw_all + g * P_GROUP_DIM:pw_all + (g + 1) * P_GROUP_DIM])
            yield
            xe = jnp.concatenate([hist_scr[:, gs], xp], axis=0)
            hist_scr[:, gs] = xp[sub - HIST:, :]
            tot = xe
            sh = 1
            while sh < w:
                tot = tot + pltpu.roll(tot, sh, axis=0)
                sh *= 2
            inv_cnt = 1.0 / jnp.minimum(tpos + 1, w).astype(F32)
            mix = (tot[HIST:, :] * inv_cnt - xp).astype(BF16)
            gate = ps_ref[:, gs] * _silu(zp)
            yield
            yb_g = (_dot(mix, pw_ref[g]) * gate).astype(BF16)
            yield
            parts[i][g + 1] = _dot(yb_g, wo_ref[D_MODEL + g * P_GROUP_DIM:D_MODEL + (g + 1) * P_GROUP_DIM, :])
            yield

    _interleave([skip_chain()] + [group_chain(g, w) for g, w in enumerate(P_WINDOWS)])
    for i, rows in enumerate(row_slices):
        acc = parts[i][0]
        for part in parts[i][1:]:
            acc = acc + part
        o_ref[rows, :] = acc


def _a_out(x2, ya, p, j, batch, seq, tm, sub):
    n = x2.shape[0]
    spt = seq // tm
    tok = lambda w: pl.BlockSpec((tm, w), lambda b, s: (b * spt + s, 0))
    names = ("norm_g", "wp", "pool_w", "pool_scale", "w_out")
    return pl.pallas_call(
        functools.partial(_a_out_kernel, sub=sub),
        grid=(batch, spt),
        in_specs=[tok(D_MODEL), tok(D_MODEL)] + [_layer_spec(p[k], j) for k in names],
        out_specs=tok(D_MODEL),
        out_shape=jax.ShapeDtypeStruct((n, D_MODEL), F32),
        scratch_shapes=[pltpu.VMEM((HIST, D_MODEL), F32)],
        compiler_params=_params(("parallel", "arbitrary")),
        name="a_pool_out_proj",
    )(x2, ya, *[p[k] for k in names])


def _layer_a(x2, p, j, batch, seq):
    ya = _a_mix(x2, p, j, batch, seq)
    return _a_out(x2, ya, p, j, batch, seq, tm=A_OUT_TILE, sub=A_OUT_SUB)


def _prep_a(norm_g, w_in, b_if, conv_w, conv_b, w_q, w_k, w_v, head_norm_g, skip, pool_w, pool_scale, w_out):
    mw = M_HEADS * M_HEAD_DIM
    g0 = 2 * mw
    g1 = g0 + 2 * M_HEADS
    z1 = g1 + mw
    wg = w_in[..., g0:g1]
    pad = [(0, 0)] * (wg.ndim - 1) + [(0, LANES - 2 * M_HEADS)]
    row = lambda a: a[:, None, :]
    return dict(
        norm_g=row(norm_g),
        w3=jnp.concatenate([w_in[..., :g0], w_in[..., g1:z1]], axis=-1).astype(BF16),
        wp=w_in[..., z1:].astype(BF16),
        wg=jnp.pad(wg, pad).astype(BF16), wgt=jnp.swapaxes(wg, -1, -2).astype(BF16),
        bg=row(jnp.pad(b_if, pad[1:])), bgt=b_if[:, :, None],
        conv_w=conv_w, conv_b=row(conv_b),
        wq=w_q.astype(BF16), wk=(w_k * (M_HEAD_DIM ** -0.5)).astype(BF16), wv=w_v.astype(BF16),
        hng=row(head_norm_g), skip=row(skip),
        pool_w=pool_w.astype(BF16), pool_scale=row(pool_scale), w_out=w_out.astype(BF16))


def _c_in_kernel(x_ref, g_ref, wcq_ref, wckv_ref, wkpe_ref, wz_ref, qng_ref, kvng_ref,
                 wuq_ref, wuk_ref, wuv_ref, qg_ref, kg_ref, tab_ref,
                 q_ref, k_ref, v_ref, sz_ref, *, sub):
    qg_n, qg_r = qg_ref[:, 0:QK_NOPE], qg_ref[:, QK_NOPE:QK_PAD]
    kg_n, kg_r = kg_ref[:, 0:QK_NOPE], kg_ref[:, QK_NOPE:QK_PAD]
    scale = QK_HEAD ** -0.5

    def rows_chain(r0):
        rows = slice(r0, r0 + sub)
        yb = _rms(x_ref[rows, :], g_ref[...]).astype(BF16)
        yield
        z = _dot(yb, wz_ref[...])
        cq = _dot(yb, wcq_ref[...])
        ckv = _dot(yb, wckv_ref[...])
        kpe = _dot(yb, wkpe_ref[...])
        yield
        cqn = _rms(cq, qng_ref[...]).astype(BF16)
        ckvn = _rms(ckv, kvng_ref[...]).astype(BF16)
        sz_ref[rows, :] = _silu(z).astype(BF16)
        tab = tab_ref[rows, :]
        q_tab = tab * qg_r
        kpe_ss = 0.5 * jnp.sum(kpe * kpe, axis=1, keepdims=True)
        kp = kpe * (tab * kg_r)
        kr = kp + pltpu.roll(kp, LANES // 2, axis=1)
        yield
        q = _dot(cqn, wuq_ref[...])
        kn = _dot(ckvn, wuk_ref[...])
        v = _dot(ckvn, wuv_ref[...]).astype(BF16)
        yield
        ones_col = (lax.broadcasted_iota(jnp.int32, (sub, V_PAD - V_HEAD), 1) == 0).astype(BF16)
        for h in range(C_HEADS):
            v_ref[rows, h * V_PAD:h * V_PAD + V_HEAD] = v[:, h * V_HEAD:(h + 1) * V_HEAD]
            v_ref[rows, h * V_PAD + V_HEAD:(h + 1) * V_PAD] = ones_col
            o = h * QK_PAD
            qn = q[:, o:o + QK_NOPE]
            qr = q[:, o + QK_NOPE:o + QK_PAD]
            ss = jnp.sum(qn * qn + 0.5 * (qr * qr), axis=1, keepdims=True)
            r = lax.rsqrt(ss * (1.0 / QK_HEAD) + RMS_EPS) * scale
            q_ref[rows, o:o + QK_NOPE] = (qn * r * qg_n).astype(BF16)
            q_ref[rows, o + QK_NOPE:o + QK_PAD] = (qr * r * q_tab).astype(BF16)
            knh = kn[:, h * QK_NOPE:(h + 1) * QK_NOPE]
            rk = lax.rsqrt((jnp.sum(knh * knh, axis=1, keepdims=True) + kpe_ss) * (1.0 / QK_HEAD) + RMS_EPS)
            k_ref[rows, o:o + QK_NOPE] = (knh * rk * kg_n).astype(BF16)
            k_ref[rows, o + QK_NOPE:o + QK_PAD] = (kr * rk).astype(BF16)
            if h % 2 == 1:
                yield

    _interleave([rows_chain(r0) for r0 in range(0, x_ref.shape[0], sub)], skew=2)


def _c_in(x2, p, j, rope_tab, tm, sub):
    n = x2.shape[0]
    tok = lambda w: pl.BlockSpec((tm, w), lambda i: (i, 0))
    names = ("norm_g", "wcq", "wckv", "wkpe", "wz", "qng", "kvng", "wuq", "wuk", "wuv", "qg", "kg")
    return pl.pallas_call(
        functools.partial(_c_in_kernel, sub=sub),
        grid=(n // tm,),
        in_specs=[tok(D_MODEL)] + [_layer_spec(p[k], j) for k in names] + [tok(LANES)],
        out_specs=[tok(C_HEADS * QK_PAD), tok(C_HEADS * QK_PAD), tok(C_HEADS * V_PAD), tok(D_MODEL)],
        out_shape=[jax.ShapeDtypeStruct((n, C_HEADS * QK_PAD), BF16),
                   jax.ShapeDtypeStruct((n, C_HEADS * QK_PAD), BF16),
                   jax.ShapeDtypeStruct((n, C_HEADS * V_PAD), BF16),
                   jax.ShapeDtypeStruct((n, D_MODEL), BF16)],
        compiler_params=_params(("parallel",)),
        name="c_in_proj",
    )(x2, *[p[k] for k in names], rope_tab)


def _attn_kernel(q_ref, k_ref, v_ref, sz_ref, o_ref, *, tq, heads):
    seq = q_ref.shape[0]
    nq = seq // tq
    rc = lax.broadcasted_iota(jnp.int32, (tq, tq), 0) // CHUNK
    cc = lax.broadcasted_iota(jnp.int32, (tq, tq), 1) // CHUNK
    mask = cc <= rc

    def row_max(blocks):
        m = blocks[0]
        for blk in blocks[1:]:
            m = jnp.maximum(m, blk)
        return jnp.max(m, axis=1, keepdims=True)

    def head_chain(h):
        qc = slice(h * QK_PAD, (h + 1) * QK_PAD)
        vc = slice(h * V_PAD, (h + 1) * V_PAD)

        def scores(qi, c):
            s = _dot_nt(q_ref[qi * tq:(qi + 1) * tq, qc], k_ref[c * tq:(c + 1) * tq, qc])
            return jnp.where(mask, s, -jnp.inf) if c == qi else s

        order = list(range(nq - 1, -1, -1))
        cur = [scores(order[0], c) for c in range(order[0] + 1)]
        yield
        for pos, qi in enumerate(order):
            m = row_max(cur)
            nq_i = order[pos + 1] if pos + 1 < nq else None
            nxt = []
            pb = []
            for c in range(qi + 1):
                pb.append(jnp.exp((cur[c] - m).astype(BF16)))
                if nq_i is not None and c <= nq_i:
                    nxt.append(scores(nq_i, c))
            p_all = pb[0] if len(pb) == 1 else jnp.concatenate(pb, axis=1)
            acc = _dot(p_all, v_ref[0:(qi + 1) * tq, vc])
            rows, oc = slice(qi * tq, (qi + 1) * tq), slice(h * V_HEAD, (h + 1) * V_HEAD)
            o = acc[:, 0:V_HEAD] * (1.0 / acc[:, V_HEAD:V_HEAD + 1])
            o_ref[rows, oc] = (o * sz_ref[rows, oc].astype(F32)).astype(BF16)
            cur = nxt
            yield

    _interleave([head_chain(h) for h in range(heads)])


def _attention(q, k, v, sz, batch, seq, tq, heads):
    n = q.shape[0]
    return pl.pallas_call(
        functools.partial(_attn_kernel, tq=tq, heads=heads),
        grid=(batch, C_HEADS // heads),
        in_specs=[pl.BlockSpec((seq, heads * QK_PAD), lambda b, h: (b, h)),
                  pl.BlockSpec((seq, heads * QK_PAD), lambda b, h: (b, h)),
                  pl.BlockSpec((seq, heads * V_PAD), lambda b, h: (b, h)),
                  pl.BlockSpec((seq, heads * V_HEAD), lambda b, h: (b, h))],
        out_specs=pl.BlockSpec((seq, heads * V_HEAD), lambda b, h: (b, h)),
        out_shape=jax.ShapeDtypeStruct((n, C_HEADS * V_HEAD), BF16),
        compiler_params=_params(("parallel", "parallel")),
        name="block_causal_attention",
    )(q, k, v, sz)


def _c_out_kernel(x_ref, o_ref, wo_ref, out_ref):
    out_ref[...] = x_ref[...] + _dot(o_ref[...], wo_ref[...])


def _c_out(x2, o, w_out, j, tm):
    n = x2.shape[0]
    tok = lambda w: pl.BlockSpec((tm, w), lambda i: (i, 0))
    return pl.pallas_call(
        _c_out_kernel,
        grid=(n // tm,),
        in_specs=[tok(D_MODEL), tok(D_MODEL), _layer_spec(w_out, j)],
        out_specs=tok(D_MODEL),
        out_shape=jax.ShapeDtypeStruct((n, D_MODEL), F32),
        compiler_params=_params(("parallel",)),
        name="c_out_proj",
    )(x2, o, w_out)


def _layer_c(x2, p, j, rope_tab, batch, seq):
    q, k, v, sz = _c_in(x2, p, j, rope_tab, tm=C_IN_TILE, sub=C_IN_SUB)
    o = _attention(q, k, v, sz, batch, seq, tq=ATTN_Q_BLOCK, heads=ATTN_HEADS_PER_STEP)
    return _c_out(x2, o, p["w_out"], j, tm=C_OUT_TILE)


def _pad_rope_cols(a):
    half = QK_ROPE // 2
    x1, x2 = a[..., :half], a[..., half:]
    return jnp.concatenate([x1, x2, x2, x1], axis=-1)


def _pad_head_cols(a):
    return jnp.concatenate([a[..., :QK_NOPE], _pad_rope_cols(a[..., QK_NOPE:])], axis=-1)


def _prep_c(norm_g, w_in, q_norm_g, kv_norm_g, w_uq, w_ukv, qn_g, kn_g, w_out):
    nl = w_in.shape[0]
    c0, c1, c2 = Q_LORA, Q_LORA + KV_LORA, Q_LORA + KV_LORA + QK_ROPE
    wuq = _pad_head_cols(w_uq.reshape(nl, Q_LORA, C_HEADS, QK_HEAD)).reshape(nl, Q_LORA, C_HEADS * QK_PAD)
    wukv = w_ukv.reshape(nl, KV_LORA, C_HEADS, QK_NOPE + V_HEAD)
    row = lambda a: a[:, None, :]
    return dict(
        norm_g=row(norm_g),
        wcq=w_in[..., :c0].astype(BF16), wckv=w_in[..., c0:c1].astype(BF16),
        wkpe=_pad_rope_cols(w_in[..., c1:c2]).astype(BF16), wz=w_in[..., c2:].astype(BF16),
        qng=row(q_norm_g), kvng=row(kv_norm_g),
        wuq=wuq.astype(BF16),
        wuk=wukv[..., :QK_NOPE].reshape(nl, KV_LORA, C_HEADS * QK_NOPE).astype(BF16),
        wuv=wukv[..., QK_NOPE:].reshape(nl, KV_LORA, C_HEADS * V_HEAD).astype(BF16),
        qg=row(_pad_head_cols(qn_g)), kg=row(_pad_head_cols(kn_g)),
        w_out=w_out.astype(BF16))


def kernel(x, positions, a_norm_g, a_w_in, a_b_if, a_conv_w, a_conv_b, a_w_q, a_w_k, a_w_v, a_head_norm_g, a_skip, a_pool_w, a_pool_scale, a_w_out, c_norm_g, c_w_in, c_q_norm_g, c_kv_norm_g, c_w_uq, c_w_ukv, c_qn_g, c_kn_g, c_w_out):
    batch, seq, d = x.shape
    depth = a_norm_g.shape[0] + c_norm_g.shape[0]
    assert d == D_MODEL and x.dtype == F32 and positions.shape == (batch, seq)
    assert a_norm_g.shape[0] == (depth + 1) // 2 and c_norm_g.shape[0] == depth // 2
    for tile in (A_MIX_TILE, A_OUT_TILE, C_IN_TILE, C_OUT_TILE, ATTN_Q_BLOCK):
        assert seq % tile == 0, (seq, tile)
    assert A_MIX_TILE % M_CHUNK == 0 and A_OUT_TILE % A_OUT_SUB == 0 and C_IN_TILE % C_IN_SUB == 0
    assert ATTN_Q_BLOCK % CHUNK == 0 and C_HEADS % ATTN_HEADS_PER_STEP == 0
    a_args = (a_norm_g, a_w_in, a_b_if, a_conv_w, a_conv_b, a_w_q, a_w_k, a_w_v, a_head_norm_g, a_skip,
              a_pool_w, a_pool_scale, a_w_out)
    c_args = (c_norm_g, c_w_in, c_q_norm_g, c_kv_norm_g, c_w_uq, c_w_ukv, c_qn_g, c_kn_g, c_w_out)
    rope_tab = _rope_table(positions)
    pa, pc = _prep_a(*a_args), _prep_c(*c_args)
    x2 = x.reshape(batch * seq, d)
    for layer in range(depth):
        j = layer // 2
        if layer % 2 == 0:
            x2 = _layer_a(x2, pa, j, batch, seq)
        else:
            x2 = _layer_c(x2, pc, j, rope_tab, batch, seq)
    return x2.reshape(batch, seq, d)
```

```python
import functools

import jax
import jax.numpy as jnp
from jax import lax
from jax.experimental import pallas as pl
from jax.experimental.pallas import tpu as pltpu

F32 = jnp.float32
BF16 = jnp.bfloat16

D_MODEL = 1024
RMS_EPS = 1e-6
CHUNK = 64
M_HEADS = 4
M_HEAD_DIM = 256
CONV_K = 4
M_CHUNK = 256
P_WINDOWS = (2, 4, 8, 16)
P_GROUP_DIM = 256
HIST = 16
CONV_HIST = 8
C_HEADS = 8
QK_NOPE = 128
QK_ROPE = 64
QK_HEAD = QK_NOPE + QK_ROPE
QK_PAD = 256
V_HEAD = 128
V_PAD = 256
Q_LORA = 384
KV_LORA = 256
ROPE_THETA = 10000.0
LANES = 128

A_MIX_TILE = 2 * M_CHUNK
A_OUT_TILE, A_OUT_SUB = 1024, 256
C_IN_TILE, C_IN_SUB = 1024, 512
ATTN_Q_BLOCK, ATTN_HEADS_PER_STEP = 256, 2
C_OUT_TILE = 1024

V7X_VMEM_BYTES = 64 * 1024 * 1024
VMEM_LIMIT = V7X_VMEM_BYTES * 7 // 8


def _params(sem):
    return pltpu.CompilerParams(dimension_semantics=sem, vmem_limit_bytes=VMEM_LIMIT)


def _const_spec(shape):
    nd = len(shape)
    return pl.BlockSpec(shape, lambda *_: (0,) * nd, pipeline_mode=pl.Buffered(1))


def _layer_spec(stacked, j):
    nd = stacked.ndim - 1
    return pl.BlockSpec((None,) + stacked.shape[1:], lambda *_: (j,) + (0,) * nd, pipeline_mode=pl.Buffered(1))


def _rms(x, g):
    return x * lax.rsqrt(jnp.mean(x * x, axis=-1, keepdims=True) + RMS_EPS) * g


def _silu(x):
    h = 0.5 * x
    return h * jnp.tanh(h) + h


def _log_sigmoid(x):
    return jnp.minimum(x, 0.0) - jnp.log1p(jnp.exp(-jnp.abs(x)))


def _dot(a, b):
    return jnp.dot(a, b, preferred_element_type=F32)


def _dot_nt(a, b):
    return lax.dot_general(a, b, (((1,), (1,)), ((), ())), preferred_element_type=F32)


def _dot_tn(a, b):
    return lax.dot_general(a, b, (((0,), (0,)), ((), ())), preferred_element_type=F32)


def _interleave(chains, skew=0):
    live = dict(enumerate(chains))
    tick = 0
    while live:
        for i in sorted(live):
            if tick >= i * skew:
                try:
                    next(live[i])
                except StopIteration:
                    del live[i]
        tick += 1


def _rope_table_kernel(pos_ref, invf_ref, tab_ref):
    half = QK_ROPE // 2
    groups = LANES // half
    ang = pos_ref[...] * invf_ref[...]
    cos, sin = jnp.cos(ang), jnp.sin(ang)
    group = lax.broadcasted_iota(jnp.int32, ang.shape, 1) // half
    rolled = {0: (cos, sin)}
    for sh in range(half, LANES, half):
        rolled[sh] = (pltpu.roll(cos, sh, axis=1), pltpu.roll(sin, sh, axis=1))
    for k in range(groups):
        at = [rolled[((g - k) * half) % LANES] for g in range(groups)]
        tab_ref[k] = jnp.where(group == 0, at[0][0],
                               jnp.where(group == 1, at[1][0],
                                         jnp.where(group == 2, -at[2][1], at[3][1])))


def _rope_table(positions):
    n = positions.size
    half = QK_ROPE // 2
    per_row = LANES // half
    rows = n // per_row
    inv_freq = ROPE_THETA ** (-jnp.arange(0, QK_ROPE, 2, dtype=F32) / QK_ROPE)
    invf = jnp.tile(inv_freq, per_row)[None, :]
    pos = positions.reshape(per_row, rows).astype(F32).T
    pos = jnp.broadcast_to(pos[:, :, None], (rows, per_row, half)).reshape(rows, LANES)
    tr = min(rows, 2048)
    assert n % per_row == 0 and rows % tr == 0, (n, tr)
    tab = pl.pallas_call(
        _rope_table_kernel,
        grid=(rows // tr,),
        in_specs=[pl.BlockSpec((tr, LANES), lambda i: (i, 0)), _const_spec((1, LANES))],
        out_specs=pl.BlockSpec((per_row, tr, LANES), lambda i: (0, i, 0)),
        out_shape=jax.ShapeDtypeStruct((per_row, rows, LANES), F32),
        compiler_params=_params(("parallel",)),
        name="rope_table",
    )(pos, invf)
    return tab.reshape(n, LANES)


def _a_mix_kernel(x_ref, g_ref, w3_ref, wg_ref, wgt_ref, bg_ref, bgt_ref, cw_ref, cb_ref,
                  wq_ref, wk_ref, wv_ref, hng_ref, skip_ref, ya_ref,
                  c_scr, n_scr, m_scr, hist_scr):
    t = M_CHUNK
    mw = M_HEADS * M_HEAD_DIM

    @pl.when(pl.program_id(1) == 0)
    def _():
        c_scr[...] = jnp.zeros_like(c_scr)
        n_scr[...] = jnp.zeros_like(n_scr)
        m_scr[...] = jnp.zeros_like(m_scr)
        hist_scr[...] = jnp.zeros_like(hist_scr)

    ri = lax.broadcasted_iota(jnp.int32, (t, t), 0)
    ci = lax.broadcasted_iota(jnp.int32, (t, t), 1)
    tril = ci <= ri
    triu = ri <= ci
    neg_inf = -jnp.inf

    def chunk_inputs(rows):
        yb = _rms(x_ref[rows, :], g_ref[...]).astype(BF16)
        graw = _dot(yb, wg_ref[...]) + bg_ref[...]
        lane = lax.broadcasted_iota(jnp.int32, graw.shape, 1)
        gcol = jnp.where(lane >= M_HEADS, _log_sigmoid(graw), graw)
        grawt = _dot_nt(wgt_ref[...], yb) + bgt_ref[...]
        row = lax.broadcasted_iota(jnp.int32, grawt.shape, 0)
        grow = jnp.where(row >= M_HEADS, _log_sigmoid(grawt), grawt)
        return yb, gcol, grow

    row_slices = [slice(r0, r0 + t) for r0 in range(0, x_ref.shape[0], t)]
    inputs = [chunk_inputs(rows) for rows in row_slices]

    def head_chain(h):
        hs = slice(h * M_HEAD_DIM, (h + 1) * M_HEAD_DIM)
        for rows, (yb, gcol, grow) in zip(row_slices, inputs):
            i_row = grow[h:h + 1, :]
            lf_row = grow[M_HEADS + h:M_HEADS + h + 1, :]
            i_col = gcol[:, h:h + 1]
            lf_col = gcol[:, M_HEADS + h:M_HEADS + h + 1]
            b_col = jnp.sum(jnp.where(tril, lf_row, 0.0), axis=1, keepdims=True)
            b_row = jnp.sum(jnp.where(triu, lf_col, 0.0), axis=0, keepdims=True)
            a_row = i_row - b_row
            a_col = i_col - b_col
            m_old = m_scr[h][:, 0:1]
            big_m = jnp.maximum(jnp.max(jnp.where(tril, a_row, neg_inf), axis=1, keepdims=True), m_old)
            decay = jnp.exp(jnp.where(tril, a_row - big_m, neg_inf))
            w_inter = jnp.exp(m_old - big_m)
            m_last = jnp.maximum(jnp.max(a_row, axis=1, keepdims=True), m_old)
            b_last = jnp.sum(lf_row, axis=1, keepdims=True)
            w_state = jnp.exp(a_col - m_last)
            wc = jnp.exp(m_old - m_last)
            inv_floor = jnp.exp(-(b_col + big_m))
            m_scr[h] = jnp.broadcast_to(b_last + m_last, (1, LANES))
            yield
            xm = _dot(yb, w3_ref[:, hs])
            yield
            xe = jnp.concatenate([hist_scr[:, hs], xm], axis=0)
            hist_scr[:, hs] = xm[t - CONV_HIST:, :]
            x1 = pltpu.roll(xe, 1, axis=0)
            u = cw_ref[3:4, hs] * xe + cw_ref[2:3, hs] * x1
            v2 = pltpu.roll(cw_ref[1:2, hs] * xe + cw_ref[0:1, hs] * x1, 2, axis=0)
            conv = cb_ref[:, hs] + (u + v2)[CONV_HIST:, :]
            xc = _silu(conv)
            xcb = xc.astype(BF16)
            yield
            q = _dot(xcb, wq_ref[h])
            k = _dot(xcb, wk_ref[h])
            v = _dot(xm.astype(BF16), wv_ref[h])
            qb, vb = q.astype(BF16), v.astype(BF16)
            yield
            s = _dot_nt(qb, k.astype(BF16)) * decay
            n_row = n_scr[h]
            den = (jnp.sum(s, axis=1, keepdims=True)
                   + w_inter * jnp.sum(q * n_row, axis=1, keepdims=True))
            yield
            num = _dot(s.astype(BF16), vb) + w_inter * _dot(qb, c_scr[h].astype(BF16))
            hh_half = num * (0.5 / jnp.maximum(jnp.abs(den), inv_floor))
            kw = k * w_state
            c_scr[h] = wc * c_scr[h] + _dot_tn(kw.astype(BF16), vb)
            n_scr[h] = wc * n_row + jnp.sum(kw, axis=0, keepdims=True)
            yield
            om = _dot(yb, w3_ref[:, mw + h * M_HEAD_DIM:mw + (h + 1) * M_HEAD_DIM])
            zm = _dot(yb, w3_ref[:, 2 * mw + h * M_HEAD_DIM:2 * mw + (h + 1) * M_HEAD_DIM])
            yield
            hm = _rms(hh_half * jnp.tanh(0.5 * om) + hh_half, hng_ref[:, hs])
            ya_ref[rows, hs] = ((hm + skip_ref[:, hs] * xc) * _silu(zm)).astype(BF16)

    _interleave([head_chain(h) for h in range(M_HEADS)])


def _a_mix(x2, p, j, batch, seq):
    n = x2.shape[0]
    t = A_MIX_TILE
    spt = seq // t
    tok = lambda w: pl.BlockSpec((t, w), lambda b, s: (b * spt + s, 0))
    names = ("norm_g", "w3", "wg", "wgt", "bg", "bgt", "conv_w", "conv_b", "wq", "wk", "wv", "hng", "skip")
    return pl.pallas_call(
        _a_mix_kernel,
        grid=(batch, spt),
        in_specs=[tok(D_MODEL)] + [_layer_spec(p[k], j) for k in names],
        out_specs=tok(D_MODEL),
        out_shape=jax.ShapeDtypeStruct((n, D_MODEL), BF16),
        scratch_shapes=[pltpu.VMEM((M_HEADS, M_HEAD_DIM, M_HEAD_DIM), F32),
                        pltpu.VMEM((M_HEADS, 1, M_HEAD_DIM), F32),
                        pltpu.VMEM((M_HEADS, 1, LANES), F32),
                        pltpu.VMEM((CONV_HIST, D_MODEL), F32)],
        compiler_params=_params(("parallel", "arbitrary")),
        name="a_mlstm",
    )(x2, *[p[k] for k in names])


def _a_out_kernel(x_ref, ya_ref, g_ref, wp_ref, pw_ref, ps_ref, wo_ref, o_ref, hist_scr, *, sub):
    tm = x_ref.shape[0]
    si = pl.program_id(1)

    @pl.when(si == 0)
    def _():
        hist_scr[...] = jnp.zeros_like(hist_scr)

    pw_all = len(P_WINDOWS) * P_GROUP_DIM
    row_slices = [slice(r0, r0 + sub) for r0 in range(0, tm, sub)]
    parts = [[None] * (len(P_WINDOWS) + 1) for _ in row_slices]
    xs = [x_ref[rows, :] for rows in row_slices]
    ybs = [_rms(x, g_ref[...]).astype(BF16) for x in xs]

    def skip_chain():
        for i, rows in enumerate(row_slices):
            parts[i][0] = xs[i] + _dot(ya_ref[rows, :], wo_ref[0:D_MODEL, :])
            yield

    def group_chain(g, w):
        gs = slice(g * P_GROUP_DIM, (g + 1) * P_GROUP_DIM)
        for i, rows in enumerate(row_slices):
            yb = ybs[i]
            tpos = si * tm + rows.start + lax.broadcasted_iota(jnp.int32, (sub, 1), 0)
            xp = _dot(yb, wp_ref[:, gs])
            zp = _dot(yb, wp_ref[:, pw_all + g * P_GROUP_DIM:pw_all + (g + 1) * P_GROUP_DIM])
            yield
            xe = jnp.concatenate([hist_scr[:, gs], xp], axis=0)
            hist_scr[:, gs] = xp[sub - HIST:, :]
            tot = xe
            sh = 1
            while sh < w:
                tot = tot + pltpu.roll(tot, sh, axis=0)
                sh *= 2
            inv_cnt = 1.0 / jnp.minimum(tpos + 1, w).astype(F32)
            mix = (tot[HIST:, :] * inv_cnt - xp).astype(BF16)
            gate = ps_ref[:, gs] * _silu(zp)
            yield
            yb_g = (_dot(mix, pw_ref[g]) * gate).astype(BF16)
            yield
            parts[i][g + 1] = _dot(yb_g, wo_ref[D_MODEL + g * P_GROUP_DIM:D_MODEL + (g + 1) * P_GROUP_DIM, :])
            yield

    _interleave([skip_chain()] + [group_chain(g, w) for g, w in enumerate(P_WINDOWS)])
    for i, rows in enumerate(row_slices):
        acc = parts[i][0]
        for part in parts[i][1:]:
            acc = acc + part
        o_ref[rows, :] = acc


def _a_out(x2, ya, p, j, batch, seq, tm, sub):
    n = x2.shape[0]
    spt = seq // tm
    tok = lambda w: pl.BlockSpec((tm, w), lambda b, s: (b * spt + s, 0))
    names = ("norm_g", "wp", "pool_w", "pool_scale", "w_out")
    return pl.pallas_call(
        functools.partial(_a_out_kernel, sub=sub),
        grid=(batch, spt),
        in_specs=[tok(D_MODEL), tok(D_MODEL)] + [_layer_spec(p[k], j) for k in names],
        out_specs=tok(D_MODEL),
        out_shape=jax.ShapeDtypeStruct((n, D_MODEL), F32),
        scratch_shapes=[pltpu.VMEM((HIST, D_MODEL), F32)],
        compiler_params=_params(("parallel", "arbitrary")),
        name="a_pool_out_proj",
    )(x2, ya, *[p[k] for k in names])


def _layer_a(x2, p, j, batch, seq):
    ya = _a_mix(x2, p, j, batch, seq)
    return _a_out(x2, ya, p, j, batch, seq, tm=A_OUT_TILE, sub=A_OUT_SUB)


def _prep_a(norm_g, w_in, b_if, conv_w, conv_b, w_q, w_k, w_v, head_norm_g, skip, pool_w, pool_scale, w_out):
    mw = M_HEADS * M_HEAD_DIM
    g0 = 2 * mw
    g1 = g0 + 2 * M_HEADS
    z1 = g1 + mw
    wg = w_in[..., g0:g1]
    pad = [(0, 0)] * (wg.ndim - 1) + [(0, LANES - 2 * M_HEADS)]
    row = lambda a: a[:, None, :]
    return dict(
        norm_g=row(norm_g),
        w3=jnp.concatenate([w_in[..., :g0], w_in[..., g1:z1]], axis=-1).astype(BF16),
        wp=w_in[..., z1:].astype(BF16),
        wg=jnp.pad(wg, pad).astype(BF16), wgt=jnp.swapaxes(wg, -1, -2).astype(BF16),
        bg=row(jnp.pad(b_if, pad[1:])), bgt=b_if[:, :, None],
        conv_w=conv_w, conv_b=row(conv_b),
        wq=w_q.astype(BF16), wk=(w_k * (M_HEAD_DIM ** -0.5)).astype(BF16), wv=w_v.astype(BF16),
        hng=row(head_norm_g), skip=row(skip),
        pool_w=pool_w.astype(BF16), pool_scale=row(pool_scale), w_out=w_out.astype(BF16))


def _c_in_kernel(x_ref, g_ref, wlow_ref, wz_ref, qng_ref, kvng_ref,
                 wuq_ref, wuk_ref, wuv_ref, qg_ref, kg_ref, tab_ref,
                 q_ref, k_ref, v_ref, sz_ref, *, sub):
    qg_n, qg_r = qg_ref[:, 0:QK_NOPE], qg_ref[:, QK_NOPE:QK_PAD]
    kg_n, kg_r = kg_ref[:, 0:QK_NOPE], kg_ref[:, QK_NOPE:QK_PAD]
    scale = QK_HEAD ** -0.5

    def rows_chain(r0):
        rows = slice(r0, r0 + sub)
        yb = _rms(x_ref[rows, :], g_ref[...]).astype(BF16)
        yield
        z = _dot(yb, wz_ref[...])
        low = _dot(yb, wlow_ref[...])
        cq = low[:, 0:Q_LORA]
        ckv = low[:, Q_LORA:Q_LORA + KV_LORA]
        kpe = low[:, Q_LORA + KV_LORA:]
        yield
        cqn = _rms(cq, qng_ref[...]).astype(BF16)
        ckvn = _rms(ckv, kvng_ref[...]).astype(BF16)
        sz_ref[rows, :] = _silu(z).astype(BF16)
        tab = tab_ref[rows, :]
        q_tab = tab * qg_r
        kpe_ss = 0.5 * jnp.sum(kpe * kpe, axis=1, keepdims=True)
        kp = kpe * (tab * kg_r)
        kr = kp + pltpu.roll(kp, LANES // 2, axis=1)
        yield
        q = _dot(cqn, wuq_ref[...])
        kn = _dot(ckvn, wuk_ref[...])
        v = _dot(ckvn, wuv_ref[...]).astype(BF16)
        yield
        ones_col = (lax.broadcasted_iota(jnp.int32, (sub, V_PAD - V_HEAD), 1) == 0).astype(BF16)
        for h in range(C_HEADS):
            v_ref[rows, h * V_PAD:h * V_PAD + V_HEAD] = v[:, h * V_HEAD:(h + 1) * V_HEAD]
            v_ref[rows, h * V_PAD + V_HEAD:(h + 1) * V_PAD] = ones_col
            o = h * QK_PAD
            qn = q[:, o:o + QK_NOPE]
            qr = q[:, o + QK_NOPE:o + QK_PAD]
            ss = jnp.sum(qn * qn + 0.5 * (qr * qr), axis=1, keepdims=True)
            r = lax.rsqrt(ss * (1.0 / QK_HEAD) + RMS_EPS) * scale
            q_ref[rows, o:o + QK_NOPE] = (qn * r * qg_n).astype(BF16)
            q_ref[rows, o + QK_NOPE:o + QK_PAD] = (qr * r * q_tab).astype(BF16)
            knh = kn[:, h * QK_NOPE:(h + 1) * QK_NOPE]
            rk = lax.rsqrt((jnp.sum(knh * knh, axis=1, keepdims=True) + kpe_ss) * (1.0 / QK_HEAD) + RMS_EPS)
            k_ref[rows, o:o + QK_NOPE] = (knh * rk * kg_n).astype(BF16)
            k_ref[rows, o + QK_NOPE:o + QK_PAD] = (kr * rk).astype(BF16)
            if h % 2 == 1:
                yield

    _interleave([rows_chain(r0) for r0 in range(0, x_ref.shape[0], sub)])


def _c_in(x2, p, j, rope_tab, tm, sub):
    n = x2.shape[0]
    tok = lambda w: pl.BlockSpec((tm, w), lambda i: (i, 0))
    names = ("norm_g", "wlow", "wz", "qng", "kvng", "wuq", "wuk", "wuv", "qg", "kg")
    return pl.pallas_call(
        functools.partial(_c_in_kernel, sub=sub),
        grid=(n // tm,),
        in_specs=[tok(D_MODEL)] + [_layer_spec(p[k], j) for k in names] + [tok(LANES)],
        out_specs=[tok(C_HEADS * QK_PAD), tok(C_HEADS * QK_PAD), tok(C_HEADS * V_PAD), tok(D_MODEL)],
        out_shape=[jax.ShapeDtypeStruct((n, C_HEADS * QK_PAD), BF16),
                   jax.ShapeDtypeStruct((n, C_HEADS * QK_PAD), BF16),
                   jax.ShapeDtypeStruct((n, C_HEADS * V_PAD), BF16),
                   jax.ShapeDtypeStruct((n, D_MODEL), BF16)],
        compiler_params=_params(("parallel",)),
        name="c_in_proj",
    )(x2, *[p[k] for k in names], rope_tab)


def _attn_kernel(q_ref, k_ref, v_ref, sz_ref, o_ref, *, tq, heads):
    seq = q_ref.shape[0]
    nq = seq // tq
    rc = lax.broadcasted_iota(jnp.int32, (tq, tq), 0) // CHUNK
    cc = lax.broadcasted_iota(jnp.int32, (tq, tq), 1) // CHUNK
    mask = cc <= rc

    def row_max(blocks):
        m = blocks[0]
        for blk in blocks[1:]:
            m = jnp.maximum(m, blk)
        return jnp.max(m, axis=1, keepdims=True)

    def head_chain(h):
        qc = slice(h * QK_PAD, (h + 1) * QK_PAD)
        vc = slice(h * V_PAD, (h + 1) * V_PAD)

        def scores(qi, c):
            s = _dot_nt(q_ref[qi * tq:(qi + 1) * tq, qc], k_ref[c * tq:(c + 1) * tq, qc])
            return jnp.where(mask, s, -jnp.inf) if c == qi else s

        order = list(range(nq - 1, -1, -1))
        cur = [scores(order[0], c) for c in range(order[0] + 1)]
        yield
        for pos, qi in enumerate(order):
            m = row_max(cur)
            nq_i = order[pos + 1] if pos + 1 < nq else None
            nxt = []
            pb = []
            for c in range(qi + 1):
                pb.append(jnp.exp((cur[c] - m).astype(BF16)))
                if nq_i is not None and c <= nq_i:
                    nxt.append(scores(nq_i, c))
            p_all = pb[0] if len(pb) == 1 else jnp.concatenate(pb, axis=1)
            acc = _dot(p_all, v_ref[0:(qi + 1) * tq, vc])
            rows, oc = slice(qi * tq, (qi + 1) * tq), slice(h * V_HEAD, (h + 1) * V_HEAD)
            o = acc[:, 0:V_HEAD] * (1.0 / acc[:, V_HEAD:V_HEAD + 1])
            o_ref[rows, oc] = (o * sz_ref[rows, oc].astype(F32)).astype(BF16)
            cur = nxt
            yield

    _interleave([head_chain(h) for h in range(heads)])


def _attention(q, k, v, sz, batch, seq, tq, heads):
    n = q.shape[0]
    return pl.pallas_call(
        functools.partial(_attn_kernel, tq=tq, heads=heads),
        grid=(batch, C_HEADS // heads),
        in_specs=[pl.BlockSpec((seq, heads * QK_PAD), lambda b, h: (b, h)),
                  pl.BlockSpec((seq, heads * QK_PAD), lambda b, h: (b, h)),
                  pl.BlockSpec((seq, heads * V_PAD), lambda b, h: (b, h)),
                  pl.BlockSpec((seq, heads * V_HEAD), lambda b, h: (b, h))],
        out_specs=pl.BlockSpec((seq, heads * V_HEAD), lambda b, h: (b, h)),
        out_shape=jax.ShapeDtypeStruct((n, C_HEADS * V_HEAD), BF16),
        compiler_params=_params(("parallel", "parallel")),
        name="block_causal_attention",
    )(q, k, v, sz)


def _c_out_kernel(x_ref, o_ref, wo_ref, out_ref):
    out_ref[...] = x_ref[...] + _dot(o_ref[...], wo_ref[...])


def _c_out(x2, o, w_out, j, tm):
    n = x2.shape[0]
    tok = lambda w: pl.BlockSpec((tm, w), lambda i: (i, 0))
    return pl.pallas_call(
        _c_out_kernel,
        grid=(n // tm,),
        in_specs=[tok(D_MODEL), tok(D_MODEL), _layer_spec(w_out, j)],
        out_specs=tok(D_MODEL),
        out_shape=jax.ShapeDtypeStruct((n, D_MODEL), F32),
        compiler_params=_params(("parallel",)),
        name="c_out_proj",
    )(x2, o, w_out)


def _layer_c(x2, p, j, rope_tab, batch, seq):
    q, k, v, sz = _c_in(x2, p, j, rope_tab, tm=C_IN_TILE, sub=C_IN_SUB)
    o = _attention(q, k, v, sz, batch, seq, tq=ATTN_Q_BLOCK, heads=ATTN_HEADS_PER_STEP)
    return _c_out(x2, o, p["w_out"], j, tm=C_OUT_TILE)


def _pad_rope_cols(a):
    half = QK_ROPE // 2
    x1, x2 = a[..., :half], a[..., half:]
    return jnp.concatenate([x1, x2, x2, x1], axis=-1)


def _pad_head_cols(a):
    return jnp.concatenate([a[..., :QK_NOPE], _pad_rope_cols(a[..., QK_NOPE:])], axis=-1)


def _prep_c(norm_g, w_in, q_norm_g, kv_norm_g, w_uq, w_ukv, qn_g, kn_g, w_out):
    nl = w_in.shape[0]
    c0, c1, c2 = Q_LORA, Q_LORA + KV_LORA, Q_LORA + KV_LORA + QK_ROPE
    wuq = _pad_head_cols(w_uq.reshape(nl, Q_LORA, C_HEADS, QK_HEAD)).reshape(nl, Q_LORA, C_HEADS * QK_PAD)
    wukv = w_ukv.reshape(nl, KV_LORA, C_HEADS, QK_NOPE + V_HEAD)
    row = lambda a: a[:, None, :]
    return dict(
        norm_g=row(norm_g),
        wlow=jnp.concatenate([w_in[..., :c1], _pad_rope_cols(w_in[..., c1:c2])], axis=-1).astype(BF16),
        wz=w_in[..., c2:].astype(BF16),
        qng=row(q_norm_g), kvng=row(kv_norm_g),
        wuq=wuq.astype(BF16),
        wuk=wukv[..., :QK_NOPE].reshape(nl, KV_LORA, C_HEADS * QK_NOPE).astype(BF16),
        wuv=wukv[..., QK_NOPE:].reshape(nl, KV_LORA, C_HEADS * V_HEAD).astype(BF16),
        qg=row(_pad_head_cols(qn_g)), kg=row(_pad_head_cols(kn_g)),
        w_out=w_out.astype(BF16))


def kernel(x, positions, a_norm_g, a_w_in, a_b_if, a_conv_w, a_conv_b, a_w_q, a_w_k, a_w_v, a_head_norm_g, a_skip, a_pool_w, a_pool_scale, a_w_out, c_norm_g, c_w_in, c_q_norm_g, c_kv_norm_g, c_w_uq, c_w_ukv, c_qn_g, c_kn_g, c_w_out):
    batch, seq, d = x.shape
    depth = a_norm_g.shape[0] + c_norm_g.shape[0]
    assert d == D_MODEL and x.dtype == F32 and positions.shape == (batch, seq)
    assert a_norm_g.shape[0] == (depth + 1) // 2 and c_norm_g.shape[0] == depth // 2
    for tile in (A_MIX_TILE, A_OUT_TILE, C_IN_TILE, C_OUT_TILE, ATTN_Q_BLOCK):
        assert seq % tile == 0, (seq, tile)
    assert A_MIX_TILE % M_CHUNK == 0 and A_OUT_TILE % A_OUT_SUB == 0 and C_IN_TILE % C_IN_SUB == 0
    assert ATTN_Q_BLOCK % CHUNK == 0 and C_HEADS % ATTN_HEADS_PER_STEP == 0
    a_args = (a_norm_g, a_w_in, a_b_if, a_conv_w, a_conv_b, a_w_q, a_w_k, a_w_v, a_head_norm_g, a_skip,
              a_pool_w, a_pool_scale, a_w_out)
    c_args = (c_norm_g, c_w_in, c_q_norm_g, c_kv_norm_g, c_w_uq, c_w_ukv, c_qn_g, c_kn_g, c_w_out)
    rope_tab = _rope_table(positions)
    pa, pc = _prep_a(*a_args), _prep_c(*c_args)
    x2 = x.reshape(batch * seq, d)
    for layer in range(depth):
        j = layer // 2
        if layer % 2 == 0:
            x2 = _layer_a(x2, pa, j, batch, seq)
        else:
            x2 = _layer_c(x2, pc, j, rope_tab, batch, seq)
    return x2.reshape(batch, seq, d)
```
